```python
import jax, jax.numpy as jnp
from jax import lax

D_MODEL = 2048
BATCH = 2
SEQ = 4096
DEPTH = 4
DEC_BATCH = 32
DEC_SEQ = 4
PAST_LEN = 16384
PAGE_SIZE = 128

HEAD_DIM = 64
N_HEADS = D_MODEL // HEAD_DIM
N_KV_HEADS = N_HEADS // 8
GQA_GROUP = N_HEADS // N_KV_HEADS
ROT_DIM = HEAD_DIM // 4
ROPE_THETA = 500000.0
WINDOW = 128
BLOCK = 128
CONV_WIDTH = 3
SGU_WIDTH = D_MODEL
SGU_GROUPS = 8
SGU_GROUP_DIM = SGU_WIDTH // SGU_GROUPS
CHUNK = 128
D_FF = 5632
N_MIXERS = 3
N_ATTN_LAYERS = (DEPTH + 2) // 3
N_CONV_LAYERS = (DEPTH + 1) // 3
N_SGU_LAYERS = DEPTH // 3
RMS_EPS = 1e-6
LN_EPS = 1e-5
NEG_INF = -1e30

kernel_name = "hybrid_swa_shortconv_sgu_macaron_step"


def rms_norm(x, g):
    xf = x.astype(jnp.float32)
    y = xf * lax.rsqrt(jnp.mean(xf * xf, axis=-1, keepdims=True) + RMS_EPS)
    return (y * g.astype(jnp.float32)).astype(x.dtype)


def layer_norm(x, g, b):
    xf = x.astype(jnp.float32)
    mu = jnp.mean(xf, axis=-1, keepdims=True)
    xc = xf - mu
    y = xc * lax.rsqrt(jnp.mean(xc * xc, axis=-1, keepdims=True) + LN_EPS)
    return (y * g.astype(jnp.float32) + b.astype(jnp.float32)).astype(x.dtype)


def swiglu(h, w_gu, w_down):
    g, u = jnp.split(h @ w_gu, 2, axis=-1)
    return (jax.nn.silu(g) * u) @ w_down


def half_ffn(x, g_pre, g_post, w_gu, w_down):
    return x + 0.5 * rms_norm(swiglu(rms_norm(x, g_pre), w_gu, w_down), g_post)


def rope(x, pos):
    half = ROT_DIM // 2
    inv = jnp.power(jnp.float32(ROPE_THETA), -jnp.arange(0, ROT_DIM, 2, dtype=jnp.float32) / ROT_DIM)
    ang = pos.astype(jnp.float32)[:, None] * inv[None, :]
    c = jnp.cos(ang)[None, :, None, :]
    s = jnp.sin(ang)[None, :, None, :]
    xf = x.astype(jnp.float32)
    x1, x2 = xf[..., :half], xf[..., half:ROT_DIM]
    out = jnp.concatenate([x1 * c - x2 * s, x2 * c + x1 * s, xf[..., ROT_DIM:]], axis=-1)
    return out.astype(x.dtype)


def qkv_proj(h, pos, w_qkv):
    B, L, _ = h.shape
    q, k, v = jnp.split(h @ w_qkv, [N_HEADS * HEAD_DIM, (N_HEADS + N_KV_HEADS) * HEAD_DIM], axis=-1)
    q = rope(q.reshape(B, L, N_HEADS, HEAD_DIM), pos)
    k = rope(k.reshape(B, L, N_KV_HEADS, HEAD_DIM), pos)
    v = v.reshape(B, L, N_KV_HEADS, HEAD_DIM)
    return q, k, v


def sink_attention(q, k, v, mask, sinks):
    s = jnp.einsum('bnqhgd,bnkhd->bnhgqk', q, k).astype(jnp.float32) * (HEAD_DIM ** -0.5)
    s = jnp.where(mask[None, :, None, None], s, NEG_INF)
    sk = sinks.astype(jnp.float32).reshape(N_KV_HEADS, GQA_GROUP)[None, None, :, :, None, None]
    m = jnp.maximum(jnp.max(s, axis=-1, keepdims=True), sk)
    p = jnp.exp(s - m)
    denom = jnp.sum(p, axis=-1, keepdims=True) + jnp.exp(sk - m)
    p = (p / denom).astype(v.dtype)
    return jnp.einsum('bnhgqk,bnkhd->bnqhgd', p, v)


def swa_prompt(h, w_qkv, w_o, sinks):
    B, L, _ = h.shape
    nb = L // BLOCK
    q, k, v = qkv_proj(h, jnp.arange(L), w_qkv)
    qb = q.reshape(B, nb, BLOCK, N_KV_HEADS, GQA_GROUP, HEAD_DIM)
    kb = k.reshape(B, nb, BLOCK, N_KV_HEADS, HEAD_DIM)
    vb = v.reshape(B, nb, BLOCK, N_KV_HEADS, HEAD_DIM)

    def with_prev(t):
        prev = jnp.concatenate([jnp.zeros_like(t[:, :1]), t[:, :-1]], axis=1)
        return jnp.concatenate([prev, t], axis=2)

    blk = jnp.arange(nb)[:, None] * BLOCK
    qpos = blk + jnp.arange(BLOCK)[None, :]
    kpos = blk - BLOCK + jnp.arange(2 * BLOCK)[None, :]
    d = qpos[:, :, None] - kpos[:, None, :]
    mask = (d >= 0) & (d <= WINDOW) & (kpos[:, None, :] >= 0)
    o = sink_attention(qb, with_prev(kb), with_prev(vb), mask, sinks)
    o = o.reshape(B, L, N_HEADS * HEAD_DIM) @ w_o
    return o, k[:, L - WINDOW:], v[:, L - WINDOW:]


def swa_sample(h, ck, cv, w_qkv, w_o, sinks):
    B, L, _ = h.shape
    q, k, v = qkv_proj(h, PAST_LEN + jnp.arange(L), w_qkv)
    kc = jnp.concatenate([ck.astype(k.dtype), k], axis=1)
    vc = jnp.concatenate([cv.astype(v.dtype), v], axis=1)
    d = (WINDOW + jnp.arange(L))[:, None] - jnp.arange(WINDOW + L)[None, :]
    mask = ((d >= 0) & (d <= WINDOW))[None]
    o = sink_attention(q.reshape(B, 1, L, N_KV_HEADS, GQA_GROUP, HEAD_DIM), kc[:, None], vc[:, None], mask, sinks)
    o = o.reshape(B, L, N_HEADS * HEAD_DIM) @ w_o
    return o, kc[:, L:], vc[:, L:]


def short_conv(h, prefix, w_in, w_conv, w_out):
    L = h.shape[1]
    b_gate, c_gate, hin = jnp.split(h @ w_in, 3, axis=-1)
    z = c_gate * hin
    zp = jnp.concatenate([prefix.astype(z.dtype), z], axis=1)
    y = w_conv[0] * zp[:, 0:L]
    for j in range(1, CONV_WIDTH):
        y = y + w_conv[j] * zp[:, j:j + L]
    return (b_gate * y) @ w_out, zp[:, L:]


def sgu(h, chunk, w_in, ln_g, ln_b, w_s, b_s, w_out):
    B, L, _ = h.shape
    nc = L // chunk
    u, v = jnp.split(jax.nn.gelu(h @ w_in), 2, axis=-1)
    v = layer_norm(v, ln_g, ln_b)
    causal = jnp.tril(jnp.ones((chunk, chunk), dtype=bool))
    ws = w_s[:, :chunk, :chunk]
    ws = jnp.where(causal[None], ws, jnp.zeros((), ws.dtype)).astype(v.dtype)
    vg = v.reshape(B, nc, chunk, SGU_GROUPS, SGU_GROUP_DIM)
    mixed = jnp.einsum('gts,bcsgd->bctgd', ws, vg) + b_s[:, :chunk].T.astype(v.dtype)[None, None, :, :, None]
    out = (u * mixed.reshape(B, L, SGU_WIDTH)) @ w_out
    return out, v


def setup_inputs(seed: int = 0) -> dict:
    key = jax.random.key(seed)
    ks = jax.random.split(key, 20)

    def nrm(k, shape, scale):
        return jax.random.normal(k, shape, jnp.float32) * scale

    D, H, KV, hd = D_MODEL, N_HEADS, N_KV_HEADS, HEAD_DIM
    return {
        "x_prompt": nrm(ks[0], (BATCH, SEQ, D), 1.0),
        "x_sample": nrm(ks[1], (DEC_BATCH, DEC_SEQ, D), 1.0),
        "cache_k": nrm(ks[2], (N_ATTN_LAYERS, DEC_BATCH, WINDOW, KV, hd), 1.0),
        "cache_v": nrm(ks[3], (N_ATTN_LAYERS, DEC_BATCH, WINDOW, KV, hd), 1.0),
        "state_conv": nrm(ks[4], (N_CONV_LAYERS, DEC_BATCH, CONV_WIDTH - 1, D), 1.0),
        "norm_g": 1.0 + nrm(ks[5], (DEPTH, 6, D), 0.02),
        "ffn_w_gu": nrm(ks[6], (DEPTH, 2, D, 2 * D_FF), D ** -0.5),
        "ffn_w_down": nrm(ks[7], (DEPTH, 2, D_FF, D), D_FF ** -0.5),
        "attn_w_qkv": nrm(ks[8], (N_ATTN_LAYERS, D, (H + 2 * KV) * hd), D ** -0.5),
        "attn_w_o": nrm(ks[9], (N_ATTN_LAYERS, H * hd, D), (H * hd) ** -0.5),
        "attn_sinks": nrm(ks[10], (N_ATTN_LAYERS, H), 1.0),
        "conv_w_in": nrm(ks[11], (N_CONV_LAYERS, D, 3 * D), D ** -0.5),
        "conv_w": nrm(ks[12], (N_CONV_LAYERS, CONV_WIDTH, D), CONV_WIDTH ** -0.5),
        "conv_w_out": nrm(ks[13], (N_CONV_LAYERS, D, D), D ** -0.5),
        "sgu_w_in": nrm(ks[14], (N_SGU_LAYERS, D, 2 * SGU_WIDTH), D ** -0.5),
        "sgu_ln_g": 1.0 + nrm(ks[15], (N_SGU_LAYERS, SGU_WIDTH), 0.02),
        "sgu_ln_b": nrm(ks[16], (N_SGU_LAYERS, SGU_WIDTH), 0.02),
        "sgu_w_s": nrm(ks[17], (N_SGU_LAYERS, SGU_GROUPS, CHUNK, CHUNK), CHUNK ** -0.5),
        "sgu_b_s": 1.0 + nrm(ks[18], (N_SGU_LAYERS, SGU_GROUPS, CHUNK), 0.1),
        "sgu_w_out": nrm(ks[19], (N_SGU_LAYERS, SGU_WIDTH, D), SGU_WIDTH ** -0.5),
    }


def reference(x_prompt, x_sample, cache_k, cache_v, state_conv, norm_g, ffn_w_gu, ffn_w_down,
              attn_w_qkv, attn_w_o, attn_sinks, conv_w_in, conv_w, conv_w_out,
              sgu_w_in, sgu_ln_g, sgu_ln_b, sgu_w_s, sgu_b_s, sgu_w_out):
    xp, xs = x_prompt, x_sample
    kp_l, vp_l, ks_l, vs_l, cp_l, cs_l, sv_l = [], [], [], [], [], [], []
    for i in range(DEPTH):
        kind, j = i % N_MIXERS, i // N_MIXERS
        g = norm_g[i]
        xp = half_ffn(xp, g[0], g[1], ffn_w_gu[i, 0], ffn_w_down[i, 0])
        xs = half_ffn(xs, g[0], g[1], ffn_w_gu[i, 0], ffn_w_down[i, 0])
        hp, hs = rms_norm(xp, g[2]), rms_norm(xs, g[2])
        if kind == 0:
            op, kp, vp = swa_prompt(hp, attn_w_qkv[j], attn_w_o[j], attn_sinks[j])
            os_, kn, vn = swa_sample(hs, cache_k[j], cache_v[j], attn_w_qkv[j], attn_w_o[j], attn_sinks[j])
            kp_l.append(kp); vp_l.append(vp); ks_l.append(kn); vs_l.append(vn)
        elif kind == 1:
            zero_prefix = jnp.zeros((hp.shape[0], CONV_WIDTH - 1, D_MODEL), hp.dtype)
            op, cp = short_conv(hp, zero_prefix, conv_w_in[j], conv_w[j], conv_w_out[j])
            os_, cn = short_conv(hs, state_conv[j], conv_w_in[j], conv_w[j], conv_w_out[j])
            cp_l.append(cp); cs_l.append(cn)
        else:
            op, _ = sgu(hp, CHUNK, sgu_w_in[j], sgu_ln_g[j], sgu_ln_b[j], sgu_w_s[j], sgu_b_s[j], sgu_w_out[j])
            os_, sv = sgu(hs, DEC_SEQ, sgu_w_in[j], sgu_ln_g[j], sgu_ln_b[j], sgu_w_s[j], sgu_b_s[j], sgu_w_out[j])
            sv_l.append(sv)
        xp = xp + rms_norm(op, g[3])
        xs = xs + rms_norm(os_, g[3])
        xp = half_ffn(xp, g[4], g[5], ffn_w_gu[i, 1], ffn_w_down[i, 1])
        xs = half_ffn(xs, g[4], g[5], ffn_w_gu[i, 1], ffn_w_down[i, 1])
    return (xp, xs, jnp.stack(kp_l), jnp.stack(vp_l), jnp.stack(ks_l), jnp.stack(vs_l),
            jnp.stack(cp_l), jnp.stack(cs_l), jnp.stack(sv_l))
```

```python
import functools

import jax
import jax.numpy as jnp
from jax import lax
from jax.experimental import pallas as pl
from jax.experimental.pallas import tpu as pltpu

D_MODEL = 2048
BATCH = 2
SEQ = 4096
DEPTH = 4
DEC_BATCH = 32
DEC_SEQ = 4
PAST_LEN = 16384
HEAD_DIM = 64
N_HEADS = D_MODEL // HEAD_DIM
N_KV_HEADS = N_HEADS // 8
ROT_DIM = HEAD_DIM // 4
ROPE_THETA = 500000.0
WINDOW = 128
BLOCK = 128
CONV_WIDTH = 3
SGU_WIDTH = D_MODEL
SGU_GROUPS = 8
SGU_GROUP_DIM = SGU_WIDTH // SGU_GROUPS
CHUNK = 128
D_FF = 5632
N_MIXERS = 3
RMS_EPS = 1e-6
LN_EPS = 1e-5
NEG_INF = -1e30

M_PROMPT = BATCH * SEQ
M_SAMPLE = DEC_BATCH * DEC_SEQ
M_ALL = M_PROMPT + M_SAMPLE
KV_DIM = N_KV_HEADS * HEAD_DIM
QKV_DIM = D_MODEL + 2 * KV_DIM

LANES = 128
SUBLANES = 8
MXU_COLS = 256
HEADS_PER_TILE = LANES // HEAD_DIM
Q_TILES = D_MODEL // LANES
TILES_PER_KV = Q_TILES // N_KV_HEADS
KEYS = 2 * BLOCK

TM = 640
TF = 512
VMEM_LIMIT = 56 * 1024 * 1024

F32 = jnp.float32
BF16 = jnp.bfloat16


def _rms(x, g):
    return x * lax.rsqrt(jnp.mean(x * x, axis=-1, keepdims=True) + RMS_EPS) * g


def _params(*sem):
    return pltpu.CompilerParams(dimension_semantics=sem, vmem_limit_bytes=VMEM_LIMIT)


def _ffn_kernel(x_ref, gpre_ref, gpost_ref, wg_ref, wu_ref, wd_ref, o_ref, h_ref, acc_ref, *, nj):
    j = pl.program_id(1)

    @pl.when(j == 0)
    def _():
        h_ref[...] = _rms(x_ref[...], gpre_ref[...]).astype(BF16)

    h = h_ref[...]
    g = jnp.dot(h, wg_ref[...], preferred_element_type=F32)
    u = jnp.dot(h, wu_ref[...], preferred_element_type=F32)
    a = (g * jax.nn.sigmoid(g) * u).astype(BF16)
    d = jnp.dot(a, wd_ref[...], preferred_element_type=F32)

    @pl.when(j == 0)
    def _():
        acc_ref[...] = d

    @pl.when(j > 0)
    def _():
        acc_ref[...] += d

    @pl.when(j == nj - 1)
    def _():
        o_ref[...] = x_ref[...] + 0.5 * _rms(acc_ref[...], gpost_ref[...])


def _ffn(x, g_pre, g_post, w_gu, w_down):
    nj = D_FF // TF
    return pl.pallas_call(
        functools.partial(_ffn_kernel, nj=nj),
        grid=(M_ALL // TM, nj),
        in_specs=[
            pl.BlockSpec((TM, D_MODEL), lambda i, j: (i, 0)),
            pl.BlockSpec((1, D_MODEL), lambda i, j: (0, 0)),
            pl.BlockSpec((1, D_MODEL), lambda i, j: (0, 0)),
            pl.BlockSpec((D_MODEL, TF), lambda i, j: (0, j)),
            pl.BlockSpec((D_MODEL, TF), lambda i, j: (0, j + nj)),
            pl.BlockSpec((TF, D_MODEL), lambda i, j: (j, 0)),
        ],
        out_specs=pl.BlockSpec((TM, D_MODEL), lambda i, j: (i, 0)),
        out_shape=jax.ShapeDtypeStruct((M_ALL, D_MODEL), F32),
        scratch_shapes=[pltpu.VMEM((TM, D_MODEL), BF16), pltpu.VMEM((TM, D_MODEL), F32)],
        compiler_params=_params("arbitrary", "arbitrary"),
        name="ffn",
    )(x, g_pre, g_post, w_gu, w_gu, w_down)


def _proj_kernel(x_ref, g_ref, w_ref, o_ref, h_ref, *, act):
    @pl.when(pl.program_id(1) == 0)
    def _():
        h_ref[...] = _rms(x_ref[...], g_ref[...]).astype(BF16)

    y = jnp.dot(h_ref[...], w_ref[...], preferred_element_type=F32)
    if act == "gelu":
        y = jax.nn.gelu(y)
    o_ref[...] = y


def _proj(x, g, w, *, act, tn):
    n = w.shape[1]
    return pl.pallas_call(
        functools.partial(_proj_kernel, act=act),
        grid=(M_ALL // TM, n // tn),
        in_specs=[
            pl.BlockSpec((TM, D_MODEL), lambda i, j: (i, 0)),
            pl.BlockSpec((1, D_MODEL), lambda i, j: (0, 0)),
            pl.BlockSpec((D_MODEL, tn), lambda i, j: (0, j)),
        ],
        out_specs=pl.BlockSpec((TM, tn), lambda i, j: (i, j)),
        out_shape=jax.ShapeDtypeStruct((M_ALL, n), F32),
        scratch_shapes=[pltpu.VMEM((TM, D_MODEL), BF16)],
        compiler_params=_params("arbitrary", "arbitrary"),
        name="proj_" + act,
    )(x, g, w)


def _outproj_kernel(x_ref, a_ref, g_ref, w_ref, o_ref):
    y = jnp.dot(a_ref[...], w_ref[...], preferred_element_type=F32)
    o_ref[...] = x_ref[...] + _rms(y, g_ref[...])


def _outproj(x, a, g, w):
    return pl.pallas_call(
        _outproj_kernel,
        grid=(M_ALL // TM,),
        in_specs=[
            pl.BlockSpec((TM, D_MODEL), lambda i: (i, 0)),
            pl.BlockSpec((TM, D_MODEL), lambda i: (i, 0)),
            pl.BlockSpec((1, D_MODEL), lambda i: (0, 0)),
            pl.BlockSpec((D_MODEL, D_MODEL), lambda i: (0, 0)),
        ],
        out_specs=pl.BlockSpec((TM, D_MODEL), lambda i: (i, 0)),
        out_shape=jax.ShapeDtypeStruct((M_ALL, D_MODEL), F32),
        compiler_params=_params("arbitrary"),
        name="outproj",
    )(x, a, g, w)


def _rope_tables():
    half = ROT_DIM // 2
    inv = jnp.power(jnp.float32(ROPE_THETA), -jnp.arange(0, ROT_DIM, 2, dtype=jnp.float32) / ROT_DIM)
    pos = jnp.concatenate([
        jnp.tile(jnp.arange(SEQ), BATCH),
        jnp.tile(PAST_LEN + jnp.arange(DEC_SEQ), DEC_BATCH),
    ])
    ang = pos.astype(jnp.float32)[:, None] * inv[None, :]
    c, s = jnp.cos(ang), jnp.sin(ang)
    pad = jnp.zeros((M_ALL, HEAD_DIM - ROT_DIM), F32)
    zero = jnp.zeros_like(s)
    cos_h = jnp.concatenate([c, c, pad + 1.0], axis=1)
    sa_h = jnp.concatenate([-s, zero, pad], axis=1)
    sb_h = jnp.concatenate([zero, s, pad], axis=1)
    rep = lambda t: jnp.tile(t, (1, HEADS_PER_TILE))
    return rep(cos_h), rep(sa_h), rep(sb_h)


def _attn_proj_kernel(x_ref, g_ref, w_ref, cos_ref, sa_ref, sb_ref, q_ref, kv_ref):
    h = _rms(x_ref[...], g_ref[...]).astype(BF16)
    cos, sa, sb = cos_ref[...], sa_ref[...], sb_ref[...]
    half = ROT_DIM // 2
    for c in range(QKV_DIM // MXU_COLS):
        y2 = jnp.dot(h, w_ref[:, c * MXU_COLS:(c + 1) * MXU_COLS], preferred_element_type=F32)
        for part in range(MXU_COLS // LANES):
            t = c * (MXU_COLS // LANES) + part
            y = y2[:, part * LANES:(part + 1) * LANES]
            if t < Q_TILES + KV_DIM // LANES:
                y = y * cos + pltpu.roll(y, LANES - half, axis=1) * sa + pltpu.roll(y, half, axis=1) * sb
            if t < Q_TILES:
                q_ref[:, t * LANES:(t + 1) * LANES] = (y * (HEAD_DIM ** -0.5)).astype(BF16)
            else:
                kv_ref[:, (t - Q_TILES) * LANES:(t - Q_TILES + 1) * LANES] = y


def _attn_proj(x, g, w, tables):
    row = lambda i: (i, 0)
    fixed = lambda i: (0, 0)
    return pl.pallas_call(
        _attn_proj_kernel,
        grid=(M_ALL // TM,),
        in_specs=[
            pl.BlockSpec((TM, D_MODEL), row),
            pl.BlockSpec((1, D_MODEL), fixed),
            pl.BlockSpec((D_MODEL, QKV_DIM), fixed),
            pl.BlockSpec((TM, LANES), row),
            pl.BlockSpec((TM, LANES), row),
            pl.BlockSpec((TM, LANES), row),
        ],
        out_specs=[pl.BlockSpec((TM, D_MODEL), row), pl.BlockSpec((TM, 2 * KV_DIM), row)],
        out_shape=[jax.ShapeDtypeStruct((M_ALL, D_MODEL), BF16),
                   jax.ShapeDtypeStruct((M_ALL, 2 * KV_DIM), F32)],
        compiler_params=_params("arbitrary"),
        name="attn_proj",
    )(x, g, w, *tables)


def _both_halves(tile, first_half):
    lane = lax.broadcasted_iota(jnp.int32, tile.shape, 1)
    rolled = pltpu.roll(tile, HEAD_DIM, axis=1)
    keep = (lane < HEAD_DIM) if first_half else (lane >= HEAD_DIM)
    return jnp.where(keep, tile, rolled)


def _block_diag(rep):
    lane = lax.broadcasted_iota(jnp.int32, rep.shape, 1)
    top = jnp.where(lane < HEAD_DIM, rep, 0.0)
    bot = jnp.where(lane >= HEAD_DIM, rep, 0.0)
    return jnp.concatenate([top, bot], axis=0).astype(BF16)


def _attend(q2, k_bd, v_bd, mask, sink_a, sink_b):
    s = lax.dot_general(q2, k_bd, (((1,), (1,)), ((), ())), preferred_element_type=F32)
    sa = jnp.where(mask, s[:, :KEYS], NEG_INF)
    sb = jnp.where(mask, s[:, KEYS:], NEG_INF)
    ma = jnp.maximum(jnp.max(sa, axis=-1, keepdims=True), sink_a)
    mb = jnp.maximum(jnp.max(sb, axis=-1, keepdims=True), sink_b)
    pa = jnp.exp(sa - ma)
    pb = jnp.exp(sb - mb)
    da = jnp.sum(pa, axis=-1, keepdims=True) + jnp.exp(sink_a - ma)
    db = jnp.sum(pb, axis=-1, keepdims=True) + jnp.exp(sink_b - mb)
    p = jnp.concatenate([pa, pb], axis=1).astype(BF16)
    o = jnp.dot(p, v_bd, preferred_element_type=F32)
    lane = lax.broadcasted_iota(jnp.int32, o.shape, 1)
    return o * jnp.where(lane < HEAD_DIM, 1.0 / da, 1.0 / db)


def _window_mask(rows, q_index, min_key):
    kj = lax.broadcasted_iota(jnp.int32, (rows, KEYS), 1)
    return (kj >= q_index) & (kj <= q_index + WINDOW) & (kj >= min_key)


def _attn_prompt_kernel(sink_ref, q_ref, kvc_ref, kvp_ref, o_ref):
    min_key = jnp.where(pl.program_id(1) == 0, BLOCK, 0)
    kv = jnp.concatenate([kvp_ref[...], kvc_ref[...]], axis=0)
    qi = lax.broadcasted_iota(jnp.int32, (BLOCK, KEYS), 0)
    mask = _window_mask(BLOCK, qi, min_key)
    for h in range(N_KV_HEADS):
        lo = (h // HEADS_PER_TILE) * LANES
        first_half = h % HEADS_PER_TILE == 0
        k_bd = _block_diag(_both_halves(kv[:, lo:lo + LANES], first_half))
        v_bd = _block_diag(_both_halves(kv[:, KV_DIM + lo:KV_DIM + lo + LANES], first_half))
        for t in range(TILES_PER_KV):
            tile = h * TILES_PER_KV + t
            cols = slice(tile * LANES, (tile + 1) * LANES)
            o = _attend(q_ref[:, cols], k_bd, v_bd, mask,
                        sink_ref[HEADS_PER_TILE * tile], sink_ref[HEADS_PER_TILE * tile + 1])
            o_ref[:, cols] = o.astype(BF16)


def _attn_prompt(q, kv, sinks):
    nb = SEQ // BLOCK
    return pl.pallas_call(
        _attn_prompt_kernel,
        grid=(BATCH, nb),
        in_specs=[
            pl.BlockSpec(memory_space=pltpu.SMEM),
            pl.BlockSpec((BLOCK, D_MODEL), lambda b, n: (b * nb + n, 0)),
            pl.BlockSpec((BLOCK, 2 * KV_DIM), lambda b, n: (b * nb + n, 0)),
            pl.BlockSpec((BLOCK, 2 * KV_DIM), lambda b, n: (b * nb + jnp.maximum(n - 1, 0), 0)),
        ],
        out_specs=pl.BlockSpec((BLOCK, D_MODEL), lambda b, n: (b * nb + n, 0)),
        out_shape=jax.ShapeDtypeStruct((M_PROMPT, D_MODEL), BF16),
        compiler_params=_params("arbitrary", "arbitrary"),
        name="attn_prompt",
    )(sinks, q, kv, kv)


SAMPLE_Q_ROWS = Q_TILES * DEC_SEQ
ROWS_PER_KV = TILES_PER_KV * DEC_SEQ


def _attn_sample_kernel(sink_ref, q_ref, ck_ref, cv_ref, kn_ref, vn_ref, o_ref, kbuf, vbuf):
    pad = jnp.zeros((KEYS - WINDOW, KV_DIM), F32)
    kbuf[WINDOW:, :] = pad
    vbuf[WINDOW:, :] = pad
    kbuf[:WINDOW, :] = ck_ref[0]
    vbuf[:WINDOW, :] = cv_ref[0]
    kbuf[WINDOW:WINDOW + DEC_SEQ, :] = kn_ref[0]
    vbuf[WINDOW:WINDOW + DEC_SEQ, :] = vn_ref[0]
    k_all = kbuf[...]
    v_all = vbuf[...]
    row = lax.broadcasted_iota(jnp.int32, (ROWS_PER_KV, KEYS), 0)
    mask = _window_mask(ROWS_PER_KV, row % DEC_SEQ, 0)
    tile_of_row = lax.broadcasted_iota(jnp.int32, (ROWS_PER_KV, 1), 0) // DEC_SEQ
    for h in range(N_KV_HEADS):
        lo = (h // HEADS_PER_TILE) * LANES
        first_half = h % HEADS_PER_TILE == 0
        k_bd = _block_diag(_both_halves(k_all[:, lo:lo + LANES], first_half))
        v_bd = _block_diag(_both_halves(v_all[:, lo:lo + LANES], first_half))
        sink_a = jnp.zeros((ROWS_PER_KV, 1), F32)
        sink_b = jnp.zeros((ROWS_PER_KV, 1), F32)
        for t in range(TILES_PER_KV):
            tile = h * TILES_PER_KV + t
            sink_a = jnp.where(tile_of_row == t, sink_ref[HEADS_PER_TILE * tile], sink_a)
            sink_b = jnp.where(tile_of_row == t, sink_ref[HEADS_PER_TILE * tile + 1], sink_b)
        rows = slice(h * ROWS_PER_KV, (h + 1) * ROWS_PER_KV)
        o_ref[0, rows, :] = _attend(q_ref[0, rows, :], k_bd, v_bd, mask, sink_a, sink_b).astype(BF16)


def _attn_sample(q_s, cache_k, cache_v, k_new, v_new, sinks):
    per_b = lambda b: (b, 0, 0)
    return pl.pallas_call(
        _attn_sample_kernel,
        grid=(DEC_BATCH,),
        in_specs=[
            pl.BlockSpec(memory_space=pltpu.SMEM),
            pl.BlockSpec((1, SAMPLE_Q_ROWS, LANES), per_b),
            pl.BlockSpec((1, WINDOW, KV_DIM), per_b),
            pl.BlockSpec((1, WINDOW, KV_DIM), per_b),
            pl.BlockSpec((1, DEC_SEQ, KV_DIM), per_b),
            pl.BlockSpec((1, DEC_SEQ, KV_DIM), per_b),
        ],
        out_specs=pl.BlockSpec((1, SAMPLE_Q_ROWS, LANES), per_b),
        out_shape=jax.ShapeDtypeStruct((DEC_BATCH, SAMPLE_Q_ROWS, LANES), BF16),
        scratch_shapes=[pltpu.VMEM((KEYS, KV_DIM), F32), pltpu.VMEM((KEYS, KV_DIM), F32)],
        compiler_params=_params("arbitrary"),
        name="attn_sample",
    )(sinks, q_s, cache_k, cache_v, k_new, v_new)


def _conv_core_kernel(*refs, seq_len, tr, z_rows, has_prefix):
    if has_prefix:
        b_ref, c_ref, hin_ref, cp_ref, hp_ref, w_ref, p1_ref, p2_ref, a_ref, z_ref = refs
    else:
        b_ref, c_ref, hin_ref, cp_ref, hp_ref, w_ref, a_ref, z_ref = refs
    z = c_ref[...] * hin_ref[...]
    zprev = cp_ref[...] * hp_ref[...]
    row = lax.broadcasted_iota(jnp.int32, (tr, 1), 0)
    t = (row + pl.program_id(0) * tr) % seq_len
    last = zprev[SUBLANES - 1:SUBLANES]
    zm1 = jnp.where(row == 0, last, pltpu.roll(z, 1, axis=0))
    zm2 = jnp.where(row == 0, zprev[SUBLANES - 2:SUBLANES - 1],
                    jnp.where(row == 1, last, pltpu.roll(z, 2, axis=0)))
    if has_prefix:
        zm1 = jnp.where(t < 1, p1_ref[...], zm1)
        zm2 = jnp.where(t < 2, p2_ref[...], zm2)
    else:
        zm1 = jnp.where(t < 1, 0.0, zm1)
        zm2 = jnp.where(t < 2, 0.0, zm2)
    y = w_ref[0:1, :] * zm2 + w_ref[1:2, :] * zm1 + w_ref[2:3, :] * z
    a_ref[...] = (b_ref[...] * y).astype(BF16)
    z_ref[...] = z[tr - z_rows:, :]


def _conv_core(bch, w_conv, *, row0, rows, seq_len, tr, prefix=None):
    blk0 = row0 // tr
    prev = lambda i, col: (jnp.maximum((row0 + i * tr) // SUBLANES - 1, 0), col)
    has_prefix = prefix is not None
    z_rows = tr if has_prefix else SUBLANES
    n_seq = rows // seq_len
    in_specs = [
        pl.BlockSpec((tr, D_MODEL), lambda i: (blk0 + i, 0)),
        pl.BlockSpec((tr, D_MODEL), lambda i: (blk0 + i, 1)),
        pl.BlockSpec((tr, D_MODEL), lambda i: (blk0 + i, 2)),
        pl.BlockSpec((SUBLANES, D_MODEL), lambda i: prev(i, 1)),
        pl.BlockSpec((SUBLANES, D_MODEL), lambda i: prev(i, 2)),
        pl.BlockSpec((CONV_WIDTH, D_MODEL), lambda i: (0, 0)),
    ]
    args = [bch, bch, bch, bch, bch, w_conv]
    if has_prefix:
        in_specs += [pl.BlockSpec((tr, D_MODEL), lambda i: (i, 0))] * 2
        args += list(prefix)
        z_spec = pl.BlockSpec((tr, D_MODEL), lambda i: (i, 0))
        z_shape = jax.ShapeDtypeStruct((rows, D_MODEL), F32)
    else:
        z_spec = pl.BlockSpec((SUBLANES, D_MODEL), lambda i: ((i * tr) // seq_len, 0))
        z_shape = jax.ShapeDtypeStruct((n_seq * SUBLANES, D_MODEL), F32)
    return pl.pallas_call(
        functools.partial(_conv_core_kernel, seq_len=seq_len, tr=tr, z_rows=z_rows, has_prefix=has_prefix),
        grid=(rows // tr,),
        in_specs=in_specs,
        out_specs=[pl.BlockSpec((tr, D_MODEL), lambda i: (i, 0)), z_spec],
        out_shape=[jax.ShapeDtypeStruct((rows, D_MODEL), BF16), z_shape],
        compiler_params=_params("arbitrary"),
        name="conv_core",
    )(*args)


def _sgu_core_kernel(u_ref, v_ref, lg_ref, lb_ref, ws_ref, bst_ref, a_ref, vln_ref):
    v = v_ref[...]
    mu = jnp.mean(v, axis=-1, keepdims=True)
    xc = v - mu
    vln = xc * lax.rsqrt(jnp.mean(xc * xc, axis=-1, keepdims=True) + LN_EPS) * lg_ref[...] + lb_ref[...]
    vln_ref[...] = vln
    vb = vln.astype(BF16)
    r = lax.broadcasted_iota(jnp.int32, (CHUNK, CHUNK), 0)
    c = lax.broadcasted_iota(jnp.int32, (CHUNK, CHUNK), 1)
    causal = r >= c
    for g in range(SGU_GROUPS):
        cols = slice(g * SGU_GROUP_DIM, (g + 1) * SGU_GROUP_DIM)
        w = jnp.where(causal, ws_ref[g], 0.0).astype(BF16)
        mixed = jnp.dot(w, vb[:, cols], preferred_element_type=F32) + bst_ref[:, g:g + 1]
        a_ref[:, cols] = (u_ref[:, cols] * mixed).astype(BF16)


def _sgu_core(uv, ln_g, ln_b, ws, bst, *, row0, rows):
    blk0 = row0 // CHUNK
    return pl.pallas_call(
        _sgu_core_kernel,
        grid=(rows // CHUNK,),
        in_specs=[
            pl.BlockSpec((CHUNK, SGU_WIDTH), lambda i: (blk0 + i, 0)),
            pl.BlockSpec((CHUNK, SGU_WIDTH), lambda i: (blk0 + i, 1)),
            pl.BlockSpec((1, SGU_WIDTH), lambda i: (0, 0)),
            pl.BlockSpec((1, SGU_WIDTH), lambda i: (0, 0)),
            pl.BlockSpec((SGU_GROUPS, CHUNK, CHUNK), lambda i: (0, 0, 0)),
            pl.BlockSpec((CHUNK, SGU_GROUPS), lambda i: (0, 0)),
        ],
        out_specs=[pl.BlockSpec((CHUNK, SGU_WIDTH), lambda i: (i, 0)),
                   pl.BlockSpec((CHUNK, SGU_WIDTH), lambda i: (i, 0))],
        out_shape=[jax.ShapeDtypeStruct((rows, SGU_WIDTH), BF16),
                   jax.ShapeDtypeStruct((rows, SGU_WIDTH), F32)],
        compiler_params=_params("arbitrary"),
        name="sgu_core",
    )(uv, uv, ln_g, ln_b, ws, bst)


def _attention_layer(x, g_pre, w_qkv, sinks, cache_k, cache_v, tables):
    q, kv = _attn_proj(x, g_pre, w_qkv, tables)
    o_p = _attn_prompt(q, kv, sinks)

    q_s = q[M_PROMPT:].reshape(DEC_BATCH, DEC_SEQ, Q_TILES, LANES).transpose(0, 2, 1, 3)
    q_s = q_s.reshape(DEC_BATCH, SAMPLE_Q_ROWS, LANES)
    kv_s = kv[M_PROMPT:].reshape(DEC_BATCH, DEC_SEQ, 2 * KV_DIM)
    k_new, v_new = kv_s[..., :KV_DIM], kv_s[..., KV_DIM:]
    ck = cache_k.reshape(DEC_BATCH, WINDOW, KV_DIM)
    cv = cache_v.reshape(DEC_BATCH, WINDOW, KV_DIM)
    o_s = _attn_sample(q_s, ck, cv, k_new, v_new, sinks)
    o_s = o_s.reshape(DEC_BATCH, Q_TILES, DEC_SEQ, LANES).transpose(0, 2, 1, 3).reshape(M_SAMPLE, D_MODEL)

    kv_p = kv[:M_PROMPT].reshape(BATCH, SEQ, 2 * KV_DIM)[:, SEQ - WINDOW:]
    kv_shape = (BATCH, WINDOW, N_KV_HEADS, HEAD_DIM)
    new_kp = kv_p[..., :KV_DIM].reshape(kv_shape)
    new_vp = kv_p[..., KV_DIM:].reshape(kv_shape)
    kv_shape = (DEC_BATCH, WINDOW, N_KV_HEADS, HEAD_DIM)
    new_ks = jnp.concatenate([ck[:, DEC_SEQ:], k_new], axis=1).reshape(kv_shape)
    new_vs = jnp.concatenate([cv[:, DEC_SEQ:], v_new], axis=1).reshape(kv_shape)
    return jnp.concatenate([o_p, o_s], axis=0), (new_kp, new_vp, new_ks, new_vs)


def _conv_layer(x, g_pre, w_in, w_conv, state):
    bch = _proj(x, g_pre, w_in, act="none", tn=1024)
    a_p, z_p = _conv_core(bch, w_conv, row0=0, rows=M_PROMPT, seq_len=SEQ, tr=256)
    zeros = jnp.zeros((DEC_BATCH, DEC_SEQ - 1, D_MODEL), F32)
    p1 = jnp.concatenate([state[:, 1:2], zeros], axis=1).reshape(M_SAMPLE, D_MODEL)
    p2 = jnp.concatenate([state[:, 0:1], state[:, 1:2], zeros[:, 1:]], axis=1).reshape(M_SAMPLE, D_MODEL)
    a_s, z_s = _conv_core(bch, w_conv, row0=M_PROMPT, rows=M_SAMPLE, seq_len=DEC_SEQ, tr=M_SAMPLE,
                          prefix=(p1, p2))
    new_cp = z_p.reshape(BATCH, SUBLANES, D_MODEL)[:, SUBLANES - (CONV_WIDTH - 1):]
    new_cs = z_s.reshape(DEC_BATCH, DEC_SEQ, D_MODEL)[:, DEC_SEQ - (CONV_WIDTH - 1):]
    return jnp.concatenate([a_p, a_s], axis=0), (new_cp, new_cs)


def _sgu_layer(x, g_pre, w_in, ln_g, ln_b, w_s, b_s):
    uv = _proj(x, g_pre, w_in, act="gelu", tn=1024)
    a_p, _ = _sgu_core(uv, ln_g, ln_b, w_s, b_s.T, row0=0, rows=M_PROMPT)
    eye = jnp.eye(DEC_BATCH, dtype=F32)
    ws_s = jnp.einsum("ab,gts->gatbs", eye, w_s[:, :DEC_SEQ, :DEC_SEQ]).reshape(SGU_GROUPS, M_SAMPLE, M_SAMPLE)
    bst_s = jnp.tile(b_s[:, :DEC_SEQ].T, (DEC_BATCH, 1))
    a_s, vln_s = _sgu_core(uv, ln_g, ln_b, ws_s, bst_s, row0=M_PROMPT, rows=M_SAMPLE)
    return jnp.concatenate([a_p, a_s], axis=0), vln_s.reshape(DEC_BATCH, DEC_SEQ, SGU_WIDTH)


def kernel(x_prompt, x_sample, cache_k, cache_v, state_conv, norm_g, ffn_w_gu, ffn_w_down,
           attn_w_qkv, attn_w_o, attn_sinks, conv_w_in, conv_w, conv_w_out,
           sgu_w_in, sgu_ln_g, sgu_ln_b, sgu_w_s, sgu_b_s, sgu_w_out):
    x = jnp.concatenate([x_prompt.reshape(M_PROMPT, D_MODEL), x_sample.reshape(M_SAMPLE, D_MODEL)], axis=0)
    tables = _rope_tables()
    gain = lambda i, k: norm_g[i, k].reshape(1, D_MODEL)
    kp_l, vp_l, ks_l, vs_l, cp_l, cs_l, sv_l = [], [], [], [], [], [], []
    for i in range(DEPTH):
        kind, j = i % N_MIXERS, i // N_MIXERS
        x = _ffn(x, gain(i, 0), gain(i, 1), ffn_w_gu[i, 0].astype(BF16), ffn_w_down[i, 0].astype(BF16))
        if kind == 0:
            a, (kp, vp, ks, vs) = _attention_layer(x, gain(i, 2), attn_w_qkv[j].astype(BF16), attn_sinks[j],
                                                   cache_k[j], cache_v[j], tables)
            kp_l.append(kp); vp_l.append(vp); ks_l.append(ks); vs_l.append(vs)
            w_out = attn_w_o[j]
        elif kind == 1:
            a, (cp, cs) = _conv_layer(x, gain(i, 2), conv_w_in[j].astype(BF16), conv_w[j], state_conv[j])
            cp_l.append(cp); cs_l.append(cs)
            w_out = conv_w_out[j]
        else:
            a, sv = _sgu_layer(x, gain(i, 2), sgu_w_in[j].astype(BF16), sgu_ln_g[j].reshape(1, SGU_WIDTH),
                               sgu_ln_b[j].reshape(1, SGU_WIDTH), sgu_w_s[j], sgu_b_s[j])
            sv_l.append(sv)
            w_out = sgu_w_out[j]
        x = _outproj(x, a, gain(i, 3), w_out.astype(BF16))
        x = _ffn(x, gain(i, 4), gain(i, 5), ffn_w_gu[i, 1].astype(BF16), ffn_w_down[i, 1].astype(BF16))
    y_prompt = x[:M_PROMPT].reshape(BATCH, SEQ, D_MODEL)
    y_sample = x[M_PROMPT:].reshape(DEC_BATCH, DEC_SEQ, D_MODEL)
    return (y_prompt, y_sample, jnp.stack(kp_l), jnp.stack(vp_l), jnp.stack(ks_l), jnp.stack(vs_l),
            jnp.stack(cp_l), jnp.stack(cs_l), jnp.stack(sv_l))
```

```python
import functools

import jax
import jax.numpy as jnp
from jax import lax
from jax.experimental import pallas as pl
from jax.experimental.pallas import tpu as pltpu

D_MODEL = 2048
BATCH = 2
SEQ = 4096
DEPTH = 4
DEC_BATCH = 32
DEC_SEQ = 4
PAST_LEN = 16384
HEAD_DIM = 64
N_HEADS = D_MODEL // HEAD_DIM
N_KV_HEADS = N_HEADS // 8
ROT_DIM = HEAD_DIM // 4
ROPE_THETA = 500000.0
WINDOW = 128
BLOCK = 128
CONV_WIDTH = 3
SGU_WIDTH = D_MODEL
SGU_GROUPS = 8
SGU_GROUP_DIM = SGU_WIDTH // SGU_GROUPS
CHUNK = 128
D_FF = 5632
N_MIXERS = 3
RMS_EPS = 1e-6
LN_EPS = 1e-5
NEG_INF = -1e30

M_PROMPT = BATCH * SEQ
M_SAMPLE = DEC_BATCH * DEC_SEQ
M_ALL = M_PROMPT + M_SAMPLE
KV_DIM = N_KV_HEADS * HEAD_DIM
QKV_DIM = D_MODEL + 2 * KV_DIM

LANES = 128
SUBLANES = 8
MXU_COLS = 256
HEADS_PER_TILE = LANES // HEAD_DIM
Q_TILES = D_MODEL // LANES
TILES_PER_KV = Q_TILES // N_KV_HEADS
KEYS = 2 * BLOCK

TM = 640
TF = 512
VMEM_LIMIT = 56 * 1024 * 1024

F32 = jnp.float32
BF16 = jnp.bfloat16


def _rms(x, g):
    return x * lax.rsqrt(jnp.mean(x * x, axis=-1, keepdims=True) + RMS_EPS) * g


def _params(*sem):
    return pltpu.CompilerParams(dimension_semantics=sem, vmem_limit_bytes=VMEM_LIMIT)


def _wspec(lead, block, idx):
    return pl.BlockSpec((None,) * len(lead) + block, lambda *g: lead + idx(*g))


def _ffn_kernel(x_ref, gpre_ref, gpost_ref, wg_ref, wu_ref, wd_ref, o_ref, h_ref, acc_ref, *, nj):
    j = pl.program_id(1)

    @pl.when(j == 0)
    def _():
        h_ref[...] = _rms(x_ref[...], gpre_ref[...]).astype(BF16)
        acc_ref[...] = jnp.zeros_like(acc_ref)

    h = h_ref[...]
    g = jnp.dot(h, wg_ref[...], preferred_element_type=F32)
    u = jnp.dot(h, wu_ref[...], preferred_element_type=F32)
    a = (g * jax.nn.sigmoid(g) * u).astype(BF16)
    acc_ref[...] += jnp.dot(a, wd_ref[...], preferred_element_type=F32)

    @pl.when(j == nj - 1)
    def _():
        o_ref[...] = x_ref[...] + _rms(acc_ref[...], 0.5 * gpost_ref[...])


def _ffn(x, g_pre, g_post, w_gu, w_down, lead):
    nj = D_FF // TF
    return pl.pallas_call(
        functools.partial(_ffn_kernel, nj=nj),
        grid=(M_ALL // TM, nj),
        in_specs=[
            pl.BlockSpec((TM, D_MODEL), lambda i, j: (i, 0)),
            pl.BlockSpec((1, D_MODEL), lambda i, j: (0, 0)),
            pl.BlockSpec((1, D_MODEL), lambda i, j: (0, 0)),
            _wspec(lead, (D_MODEL, TF), lambda i, j: (0, j)),
            _wspec(lead, (D_MODEL, TF), lambda i, j: (0, j + nj)),
            _wspec(lead, (TF, D_MODEL), lambda i, j: (j, 0)),
        ],
        out_specs=pl.BlockSpec((TM, D_MODEL), lambda i, j: (i, 0)),
        out_shape=jax.ShapeDtypeStruct((M_ALL, D_MODEL), F32),
        scratch_shapes=[pltpu.VMEM((TM, D_MODEL), BF16), pltpu.VMEM((TM, D_MODEL), F32)],
        compiler_params=_params("arbitrary", "arbitrary"),
        name="ffn",
    )(x, g_pre, g_post, w_gu, w_gu, w_down)


def _proj_kernel(x_ref, g_ref, w_ref, o_ref, h_ref, *, act):
    @pl.when(pl.program_id(1) == 0)
    def _():
        h_ref[...] = _rms(x_ref[...], g_ref[...]).astype(BF16)

    y = jnp.dot(h_ref[...], w_ref[...], preferred_element_type=F32)
    if act == "gelu":
        y = jax.nn.gelu(y)
    o_ref[...] = y


def _proj(x, g, w, lead, *, act, tn):
    n = w.shape[-1]
    return pl.pallas_call(
        functools.partial(_proj_kernel, act=act),
        grid=(M_ALL // TM, n // tn),
        in_specs=[
            pl.BlockSpec((TM, D_MODEL), lambda i, j: (i, 0)),
            pl.BlockSpec((1, D_MODEL), lambda i, j: (0, 0)),
            _wspec(lead, (D_MODEL, tn), lambda i, j: (0, j)),
        ],
        out_specs=pl.BlockSpec((TM, tn), lambda i, j: (i, j)),
        out_shape=jax.ShapeDtypeStruct((M_ALL, n), F32),
        scratch_shapes=[pltpu.VMEM((TM, D_MODEL), BF16)],
        compiler_params=_params("arbitrary", "arbitrary"),
        name="proj_" + act,
    )(x, g, w)


def _outproj_kernel(x_ref, a_ref, g_ref, w_ref, o_ref):
    y = jnp.dot(a_ref[...], w_ref[...], preferred_element_type=F32)
    o_ref[...] = x_ref[...] + _rms(y, g_ref[...])


def _outproj(x, a, g, w, lead):
    return pl.pallas_call(
        _outproj_kernel,
        grid=(M_ALL // TM,),
        in_specs=[
            pl.BlockSpec((TM, D_MODEL), lambda i: (i, 0)),
            pl.BlockSpec((TM, D_MODEL), lambda i: (i, 0)),
            pl.BlockSpec((1, D_MODEL), lambda i: (0, 0)),
            _wspec(lead, (D_MODEL, D_MODEL), lambda i: (0, 0)),
        ],
        out_specs=pl.BlockSpec((TM, D_MODEL), lambda i: (i, 0)),
        out_shape=jax.ShapeDtypeStruct((M_ALL, D_MODEL), F32),
        compiler_params=_params("arbitrary"),
        name="outproj",
    )(x, a, g, w)


def _rope_tables():
    inv = jnp.power(jnp.float32(ROPE_THETA), -jnp.arange(0, ROT_DIM, 2, dtype=jnp.float32) / ROT_DIM)
    pos = jnp.concatenate([
        jnp.tile(jnp.arange(SEQ), BATCH),
        jnp.tile(PAST_LEN + jnp.arange(DEC_SEQ), DEC_BATCH),
    ])
    ang = pos.astype(jnp.float32)[:, None] * inv[None, :]
    c, s = jnp.cos(ang), jnp.sin(ang)
    pad = jnp.zeros((M_ALL, HEAD_DIM - ROT_DIM), F32)
    zero = jnp.zeros_like(s)
    cos_h = jnp.concatenate([c, c, pad + 1.0], axis=1)
    sa_h = jnp.concatenate([-s, zero, pad], axis=1)
    sb_h = jnp.concatenate([zero, s, pad], axis=1)
    rep = lambda t: jnp.tile(t, (1, HEADS_PER_TILE))
    return rep(cos_h), rep(sa_h), rep(sb_h)


def _attn_proj_kernel(x_ref, g_ref, w_ref, cos_ref, sa_ref, sb_ref, q_ref, kv_ref):
    h = _rms(x_ref[...], g_ref[...]).astype(BF16)
    cos, sa, sb = cos_ref[...], sa_ref[...], sb_ref[...]
    half = ROT_DIM // 2
    for c in range(QKV_DIM // MXU_COLS):
        y2 = jnp.dot(h, w_ref[:, c * MXU_COLS:(c + 1) * MXU_COLS], preferred_element_type=F32)
        for part in range(MXU_COLS // LANES):
            t = c * (MXU_COLS // LANES) + part
            y = y2[:, part * LANES:(part + 1) * LANES]
            if t < Q_TILES + KV_DIM // LANES:
                y = y * cos + pltpu.roll(y, LANES - half, axis=1) * sa + pltpu.roll(y, half, axis=1) * sb
            if t < Q_TILES:
                q_ref[:, t * LANES:(t + 1) * LANES] = (y * (HEAD_DIM ** -0.5)).astype(BF16)
            else:
                kv_ref[:, (t - Q_TILES) * LANES:(t - Q_TILES + 1) * LANES] = y


def _attn_proj(x, g, w, lead, tables):
    row = lambda i: (i, 0)
    fixed = lambda i: (0, 0)
    return pl.pallas_call(
        _attn_proj_kernel,
        grid=(M_ALL // TM,),
        in_specs=[
            pl.BlockSpec((TM, D_MODEL), row),
            pl.BlockSpec((1, D_MODEL), fixed),
            _wspec(lead, (D_MODEL, QKV_DIM), fixed),
            pl.BlockSpec((TM, LANES), row),
            pl.BlockSpec((TM, LANES), row),
            pl.BlockSpec((TM, LANES), row),
        ],
        out_specs=[pl.BlockSpec((TM, D_MODEL), row), pl.BlockSpec((TM, 2 * KV_DIM), row)],
        out_shape=[jax.ShapeDtypeStruct((M_ALL, D_MODEL), BF16),
                   jax.ShapeDtypeStruct((M_ALL, 2 * KV_DIM), F32)],
        compiler_params=_params("arbitrary"),
        name="attn_proj",
    )(x, g, w, *tables)


def _both_halves(tile, first_half):
    lane = lax.broadcasted_iota(jnp.int32, tile.shape, 1)
    rolled = pltpu.roll(tile, HEAD_DIM, axis=1)
    keep = (lane < HEAD_DIM) if first_half else (lane >= HEAD_DIM)
    return jnp.where(keep, tile, rolled)


def _block_diag(rep):
    lane = lax.broadcasted_iota(jnp.int32, rep.shape, 1)
    top = jnp.where(lane < HEAD_DIM, rep, 0.0)
    bot = jnp.where(lane >= HEAD_DIM, rep, 0.0)
    return jnp.concatenate([top, bot], axis=0).astype(BF16)


def _kv_block_diag(k_all, v_all, h):
    lo = (h // HEADS_PER_TILE) * LANES
    first_half = h % HEADS_PER_TILE == 0
    return (_block_diag(_both_halves(k_all[:, lo:lo + LANES], first_half)),
            _block_diag(_both_halves(v_all[:, lo:lo + LANES], first_half)))


def _attend(q, k_bd, v_bd, mask, sinks):
    rows = mask.shape[0]
    s_all = lax.dot_general(q, k_bd, (((1,), (1,)), ((), ())), preferred_element_type=F32)
    ps, invs = [], []
    for t, (sink_a, sink_b) in enumerate(sinks):
        s = s_all[t * rows:(t + 1) * rows]
        sa = jnp.where(mask, s[:, :KEYS], NEG_INF)
        sb = jnp.where(mask, s[:, KEYS:], NEG_INF)
        ma = jnp.maximum(jnp.max(sa, axis=-1, keepdims=True), sink_a)
        mb = jnp.maximum(jnp.max(sb, axis=-1, keepdims=True), sink_b)
        pa = jnp.exp(sa - ma)
        pb = jnp.exp(sb - mb)
        da = jnp.sum(pa, axis=-1, keepdims=True) + jnp.exp(sink_a - ma)
        db = jnp.sum(pb, axis=-1, keepdims=True) + jnp.exp(sink_b - mb)
        ps.append(jnp.concatenate([pa, pb], axis=1).astype(BF16))
        lane = lax.broadcasted_iota(jnp.int32, (rows, LANES), 1)
        invs.append(1.0 / jnp.where(lane < HEAD_DIM, da, db))
    o = jnp.dot(jnp.concatenate(ps, axis=0), v_bd, preferred_element_type=F32)
    return o * jnp.concatenate(invs, axis=0)


def _window_mask(rows, q_index, min_key):
    kj = lax.broadcasted_iota(jnp.int32, (rows, KEYS), 1)
    return (kj >= q_index) & (kj <= q_index + WINDOW) & (kj >= min_key)


def _attn_prompt_kernel(sink_ref, q_ref, kvc_ref, kvp_ref, o_ref):
    min_key = jnp.where(pl.program_id(1) == 0, BLOCK, 0)
    kv = jnp.concatenate([kvp_ref[...], kvc_ref[...]], axis=0)
    qi = lax.broadcasted_iota(jnp.int32, (BLOCK, KEYS), 0)
    mask = _window_mask(BLOCK, qi, min_key)
    for h in range(N_KV_HEADS):
        k_bd, v_bd = _kv_block_diag(kv[:, :KV_DIM], kv[:, KV_DIM:], h)
        tiles = range(h * TILES_PER_KV, (h + 1) * TILES_PER_KV)
        q = jnp.concatenate([q_ref[:, t * LANES:(t + 1) * LANES] for t in tiles], axis=0)
        sinks = [(sink_ref[HEADS_PER_TILE * t], sink_ref[HEADS_PER_TILE * t + 1]) for t in tiles]
        o = _attend(q, k_bd, v_bd, mask, sinks).astype(BF16)
        for n, t in enumerate(tiles):
            o_ref[:, t * LANES:(t + 1) * LANES] = o[n * BLOCK:(n + 1) * BLOCK]


def _attn_prompt(q, kv, sinks):
    nb = SEQ // BLOCK
    return pl.pallas_call(
        _attn_prompt_kernel,
        grid=(BATCH, nb),
        in_specs=[
            pl.BlockSpec(memory_space=pltpu.SMEM),
            pl.BlockSpec((BLOCK, D_MODEL), lambda b, n: (b * nb + n, 0)),
            pl.BlockSpec((BLOCK, 2 * KV_DIM), lambda b, n: (b * nb + n, 0)),
            pl.BlockSpec((BLOCK, 2 * KV_DIM), lambda b, n: (b * nb + jnp.maximum(n - 1, 0), 0)),
        ],
        out_specs=pl.BlockSpec((BLOCK, D_MODEL), lambda b, n: (b * nb + n, 0)),
        out_shape=jax.ShapeDtypeStruct((M_ALL, D_MODEL), BF16),
        compiler_params=_params("arbitrary", "arbitrary"),
        name="attn_prompt",
    )(sinks, q, kv, kv)


SAMPLE_Q_ROWS = Q_TILES * DEC_SEQ
ROWS_PER_KV = TILES_PER_KV * DEC_SEQ


def _attn_sample_kernel(sink_ref, q_ref, ck_ref, cv_ref, kn_ref, vn_ref, o_ref, kbuf, vbuf):
    pad = jnp.zeros((KEYS - WINDOW, KV_DIM), F32)
    kbuf[WINDOW:, :] = pad
    vbuf[WINDOW:, :] = pad
    kbuf[:WINDOW, :] = ck_ref[0]
    vbuf[:WINDOW, :] = cv_ref[0]
    kbuf[WINDOW:WINDOW + DEC_SEQ, :] = kn_ref[0]
    vbuf[WINDOW:WINDOW + DEC_SEQ, :] = vn_ref[0]
    k_all = kbuf[...]
    v_all = vbuf[...]
    row = lax.broadcasted_iota(jnp.int32, (ROWS_PER_KV, KEYS), 0)
    mask = _window_mask(ROWS_PER_KV, row % DEC_SEQ, 0)
    tile_of_row = lax.broadcasted_iota(jnp.int32, (ROWS_PER_KV, 1), 0) // DEC_SEQ
    for h in range(N_KV_HEADS):
        k_bd, v_bd = _kv_block_diag(k_all, v_all, h)
        sink_a = jnp.zeros((ROWS_PER_KV, 1), F32)
        sink_b = jnp.zeros((ROWS_PER_KV, 1), F32)
        for t in range(TILES_PER_KV):
            tile = h * TILES_PER_KV + t
            sink_a = jnp.where(tile_of_row == t, sink_ref[HEADS_PER_TILE * tile], sink_a)
            sink_b = jnp.where(tile_of_row == t, sink_ref[HEADS_PER_TILE * tile + 1], sink_b)
        rows = slice(h * ROWS_PER_KV, (h + 1) * ROWS_PER_KV)
        o_ref[0, rows, :] = _attend(q_ref[0, rows, :], k_bd, v_bd, mask, [(sink_a, sink_b)]).astype(BF16)


def _attn_sample(q_s, cache_k, cache_v, k_new, v_new, sinks):
    per_b = lambda b: (b, 0, 0)
    return pl.pallas_call(
        _attn_sample_kernel,
        grid=(DEC_BATCH,),
        in_specs=[
            pl.BlockSpec(memory_space=pltpu.SMEM),
            pl.BlockSpec((1, SAMPLE_Q_ROWS, LANES), per_b),
            pl.BlockSpec((1, WINDOW, KV_DIM), per_b),
            pl.BlockSpec((1, WINDOW, KV_DIM), per_b),
            pl.BlockSpec((1, DEC_SEQ, KV_DIM), per_b),
            pl.BlockSpec((1, DEC_SEQ, KV_DIM), per_b),
        ],
        out_specs=pl.BlockSpec((1, SAMPLE_Q_ROWS, LANES), per_b),
        out_shape=jax.ShapeDtypeStruct((DEC_BATCH, SAMPLE_Q_ROWS, LANES), BF16),
        scratch_shapes=[pltpu.VMEM((KEYS, KV_DIM), F32), pltpu.VMEM((KEYS, KV_DIM), F32)],
        compiler_params=_params("arbitrary"),
        name="attn_sample",
    )(sinks, q_s, cache_k, cache_v, k_new, v_new)


def _conv_core_kernel(*refs, seq_len, tr, z_rows, has_prefix):
    if has_prefix:
        b_ref, c_ref, hin_ref, cp_ref, hp_ref, w_ref, p1_ref, p2_ref, a_ref, z_ref = refs
    else:
        b_ref, c_ref, hin_ref, cp_ref, hp_ref, w_ref, a_ref, z_ref = refs
    z = c_ref[...] * hin_ref[...]
    zprev = cp_ref[...] * hp_ref[...]
    row = lax.broadcasted_iota(jnp.int32, (tr, 1), 0)
    t = (row + pl.program_id(0) * tr) % seq_len
    last = zprev[SUBLANES - 1:SUBLANES]
    zm1 = jnp.where(row == 0, last, pltpu.roll(z, 1, axis=0))
    zm2 = jnp.where(row == 0, zprev[SUBLANES - 2:SUBLANES - 1],
                    jnp.where(row == 1, last, pltpu.roll(z, 2, axis=0)))
    if has_prefix:
        zm1 = jnp.where(t < 1, p1_ref[...], zm1)
        zm2 = jnp.where(t < 2, p2_ref[...], zm2)
    else:
        zm1 = jnp.where(t < 1, 0.0, zm1)
        zm2 = jnp.where(t < 2, 0.0, zm2)
    y = w_ref[0:1, :] * zm2 + w_ref[1:2, :] * zm1 + w_ref[2:3, :] * z
    a_ref[...] = (b_ref[...] * y).astype(BF16)
    z_ref[...] = z[tr - z_rows:, :]


def _conv_core(bch, w_conv, *, row0, rows, out_rows, seq_len, tr, prefix=None):
    blk0 = row0 // tr
    prev = lambda i, col: (jnp.maximum((row0 + i * tr) // SUBLANES - 1, 0), col)
    has_prefix = prefix is not None
    z_rows = tr if has_prefix else SUBLANES
    n_seq = rows // seq_len
    in_specs = [
        pl.BlockSpec((tr, D_MODEL), lambda i: (blk0 + i, 0)),
        pl.BlockSpec((tr, D_MODEL), lambda i: (blk0 + i, 1)),
        pl.BlockSpec((tr, D_MODEL), lambda i: (blk0 + i, 2)),
        pl.BlockSpec((SUBLANES, D_MODEL), lambda i: prev(i, 1)),
        pl.BlockSpec((SUBLANES, D_MODEL), lambda i: prev(i, 2)),
        pl.BlockSpec((CONV_WIDTH, D_MODEL), lambda i: (0, 0)),
    ]
    args = [bch, bch, bch, bch, bch, w_conv]
    if has_prefix:
        in_specs += [pl.BlockSpec((tr, D_MODEL), lambda i: (i, 0))] * 2
        args += list(prefix)
        z_spec = pl.BlockSpec((tr, D_MODEL), lambda i: (i, 0))
        z_shape = jax.ShapeDtypeStruct((rows, D_MODEL), F32)
    else:
        z_spec = pl.BlockSpec((SUBLANES, D_MODEL), lambda i: ((i * tr) // seq_len, 0))
        z_shape = jax.ShapeDtypeStruct((n_seq * SUBLANES, D_MODEL), F32)
    return pl.pallas_call(
        functools.partial(_conv_core_kernel, seq_len=seq_len, tr=tr, z_rows=z_rows, has_prefix=has_prefix),
        grid=(rows // tr,),
        in_specs=in_specs,
        out_specs=[pl.BlockSpec((tr, D_MODEL), lambda i: (i, 0)), z_spec],
        out_shape=[jax.ShapeDtypeStruct((out_rows, D_MODEL), BF16), z_shape],
        compiler_params=_params("arbitrary"),
        name="conv_core",
    )(*args)


def _sgu_core_kernel(u_ref, v_ref, lg_ref, lb_ref, ws_ref, bst_ref, a_ref, *vln_ref):
    v = v_ref[...]
    mu = jnp.mean(v, axis=-1, keepdims=True)
    xc = v - mu
    vln = xc * lax.rsqrt(jnp.mean(xc * xc, axis=-1, keepdims=True) + LN_EPS) * lg_ref[...] + lb_ref[...]
    if vln_ref:
        vln_ref[0][...] = vln
    vb = vln.astype(BF16)
    r = lax.broadcasted_iota(jnp.int32, (CHUNK, CHUNK), 0)
    c = lax.broadcasted_iota(jnp.int32, (CHUNK, CHUNK), 1)
    causal = r >= c
    for g in range(SGU_GROUPS):
        cols = slice(g * SGU_GROUP_DIM, (g + 1) * SGU_GROUP_DIM)
        w = jnp.where(causal, ws_ref[g], 0.0).astype(BF16)
        mixed = jnp.dot(w, vb[:, cols], preferred_element_type=F32) + bst_ref[:, g:g + 1]
        a_ref[:, cols] = (u_ref[:, cols] * mixed).astype(BF16)


def _sgu_core(uv, ln_g, ln_b, ws, bst, *, row0, rows, out_rows, want_vln):
    blk0 = row0 // CHUNK
    out_specs = [pl.BlockSpec((CHUNK, SGU_WIDTH), lambda i: (i, 0))]
    out_shape = [jax.ShapeDtypeStruct((out_rows, SGU_WIDTH), BF16)]
    if want_vln:
        out_specs.append(pl.BlockSpec((CHUNK, SGU_WIDTH), lambda i: (i, 0)))
        out_shape.append(jax.ShapeDtypeStruct((rows, SGU_WIDTH), F32))
    return pl.pallas_call(
        _sgu_core_kernel,
        grid=(rows // CHUNK,),
        in_specs=[
            pl.BlockSpec((CHUNK, SGU_WIDTH), lambda i: (blk0 + i, 0)),
            pl.BlockSpec((CHUNK, SGU_WIDTH), lambda i: (blk0 + i, 1)),
            pl.BlockSpec((1, SGU_WIDTH), lambda i: (0, 0)),
            pl.BlockSpec((1, SGU_WIDTH), lambda i: (0, 0)),
            pl.BlockSpec((SGU_GROUPS, CHUNK, CHUNK), lambda i: (0, 0, 0)),
            pl.BlockSpec((CHUNK, SGU_GROUPS), lambda i: (0, 0)),
        ],
        out_specs=out_specs,
        out_shape=out_shape,
        compiler_params=_params("arbitrary"),
        name="sgu_core",
    )(uv, uv, ln_g, ln_b, ws, bst)


def _with_sample_rows(a_prompt, a_sample):
    return lax.dynamic_update_slice(a_prompt, a_sample, (M_PROMPT, 0))


def _attention_layer(x, g_pre, w_qkv, lead, sinks, cache_k, cache_v, tables):
    q, kv = _attn_proj(x, g_pre, w_qkv, lead, tables)
    o_p = _attn_prompt(q, kv, sinks)

    q_s = q[M_PROMPT:].reshape(DEC_BATCH, DEC_SEQ, Q_TILES, LANES).transpose(0, 2, 1, 3)
    q_s = q_s.reshape(DEC_BATCH, SAMPLE_Q_ROWS, LANES)
    kv_s = kv[M_PROMPT:].reshape(DEC_BATCH, DEC_SEQ, 2 * KV_DIM)
    k_new, v_new = kv_s[..., :KV_DIM], kv_s[..., KV_DIM:]
    ck = cache_k.reshape(DEC_BATCH, WINDOW, KV_DIM)
    cv = cache_v.reshape(DEC_BATCH, WINDOW, KV_DIM)
    o_s = _attn_sample(q_s, ck, cv, k_new, v_new, sinks)
    o_s = o_s.reshape(DEC_BATCH, Q_TILES, DEC_SEQ, LANES).transpose(0, 2, 1, 3).reshape(M_SAMPLE, D_MODEL)

    kv_p = kv[:M_PROMPT].reshape(BATCH, SEQ, 2 * KV_DIM)[:, SEQ - WINDOW:]
    kv_shape = (BATCH, WINDOW, N_KV_HEADS, HEAD_DIM)
    new_kp = kv_p[..., :KV_DIM].reshape(kv_shape)
    new_vp = kv_p[..., KV_DIM:].reshape(kv_shape)
    kv_shape = (DEC_BATCH, WINDOW, N_KV_HEADS, HEAD_DIM)
    new_ks = jnp.concatenate([ck[:, DEC_SEQ:], k_new], axis=1).reshape(kv_shape)
    new_vs = jnp.concatenate([cv[:, DEC_SEQ:], v_new], axis=1).reshape(kv_shape)
    return _with_sample_rows(o_p, o_s), (new_kp, new_vp, new_ks, new_vs)


def _conv_layer(x, g_pre, w_in, lead, w_conv, state):
    bch = _proj(x, g_pre, w_in, lead, act="none", tn=1024)
    a_p, z_p = _conv_core(bch, w_conv, row0=0, rows=M_PROMPT, out_rows=M_ALL, seq_len=SEQ, tr=256)
    zeros = jnp.zeros((DEC_BATCH, DEC_SEQ - 1, D_MODEL), F32)
    p1 = jnp.concatenate([state[:, 1:2], zeros], axis=1).reshape(M_SAMPLE, D_MODEL)
    p2 = jnp.concatenate([state[:, 0:1], state[:, 1:2], zeros[:, 1:]], axis=1).reshape(M_SAMPLE, D_MODEL)
    a_s, z_s = _conv_core(bch, w_conv, row0=M_PROMPT, rows=M_SAMPLE, out_rows=M_SAMPLE, seq_len=DEC_SEQ,
                          tr=M_SAMPLE, prefix=(p1, p2))
    new_cp = z_p.reshape(BATCH, SUBLANES, D_MODEL)[:, SUBLANES - (CONV_WIDTH - 1):]
    new_cs = z_s.reshape(DEC_BATCH, DEC_SEQ, D_MODEL)[:, DEC_SEQ - (CONV_WIDTH - 1):]
    return _with_sample_rows(a_p, a_s), (new_cp, new_cs)


def _sgu_layer(x, g_pre, w_in, lead, ln_g, ln_b, w_s, b_s):
    uv = _proj(x, g_pre, w_in, lead, act="gelu", tn=1024)
    (a_p,) = _sgu_core(uv, ln_g, ln_b, w_s, b_s.T, row0=0, rows=M_PROMPT, out_rows=M_ALL, want_vln=False)
    eye = jnp.eye(DEC_BATCH, dtype=F32)
    ws_s = jnp.einsum("ab,gts->gatbs", eye, w_s[:, :DEC_SEQ, :DEC_SEQ]).reshape(SGU_GROUPS, M_SAMPLE, M_SAMPLE)
    bst_s = jnp.tile(b_s[:, :DEC_SEQ].T, (DEC_BATCH, 1))
    a_s, vln_s = _sgu_core(uv, ln_g, ln_b, ws_s, bst_s, row0=M_PROMPT, rows=M_SAMPLE, out_rows=M_SAMPLE,
                           want_vln=True)
    return _with_sample_rows(a_p, a_s), vln_s.reshape(DEC_BATCH, DEC_SEQ, SGU_WIDTH)


def kernel(x_prompt, x_sample, cache_k, cache_v, state_conv, norm_g, ffn_w_gu, ffn_w_down,
           attn_w_qkv, attn_w_o, attn_sinks, conv_w_in, conv_w, conv_w_out,
           sgu_w_in, sgu_ln_g, sgu_ln_b, sgu_w_s, sgu_b_s, sgu_w_out):
    x = jnp.concatenate([x_prompt.reshape(M_PROMPT, D_MODEL), x_sample.reshape(M_SAMPLE, D_MODEL)], axis=0)
    tables = _rope_tables()
    gain = lambda i, k: norm_g[i, k].reshape(1, D_MODEL)
    cast = lambda w: w.astype(BF16)
    ffn_w_gu, ffn_w_down = cast(ffn_w_gu), cast(ffn_w_down)
    attn_w_qkv, attn_w_o = cast(attn_w_qkv), cast(attn_w_o)
    conv_w_in, conv_w_out = cast(conv_w_in), cast(conv_w_out)
    sgu_w_in, sgu_w_out = cast(sgu_w_in), cast(sgu_w_out)
    kp_l, vp_l, ks_l, vs_l, cp_l, cs_l, sv_l = [], [], [], [], [], [], []
    for i in range(DEPTH):
        kind, j = i % N_MIXERS, i // N_MIXERS
        x = _ffn(x, gain(i, 0), gain(i, 1), ffn_w_gu, ffn_w_down, (i, 0))
        if kind == 0:
            a, (kp, vp, ks, vs) = _attention_layer(x, gain(i, 2), attn_w_qkv, (j,), attn_sinks[j],
                                                   cache_k[j], cache_v[j], tables)
            kp_l.append(kp); vp_l.append(vp); ks_l.append(ks); vs_l.append(vs)
            w_out = attn_w_o
        elif kind == 1:
            a, (cp, cs) = _conv_layer(x, gain(i, 2), conv_w_in, (j,), conv_w[j], state_conv[j])
            cp_l.append(cp); cs_l.append(cs)
            w_out = conv_w_out
        else:
            a, sv = _sgu_layer(x, gain(i, 2), sgu_w_in, (j,), sgu_ln_g[j].reshape(1, SGU_WIDTH),
                               sgu_ln_b[j].reshape(1, SGU_WIDTH), sgu_w_s[j], sgu_b_s[j])
            sv_l.append(sv)
            w_out = sgu_w_out
        x = _outproj(x, a, gain(i, 3), w_out, (j,))
        x = _ffn(x, gain(i, 4), gain(i, 5), ffn_w_gu, ffn_w_down, (i, 1))
    y_prompt = x[:M_PROMPT].reshape(BATCH, SEQ, D_MODEL)
    y_sample = x[M_PROMPT:].reshape(DEC_BATCH, DEC_SEQ, D_MODEL)
    return (y_prompt, y_sample, jnp.stack(kp_l), jnp.stack(vp_l), jnp.stack(ks_l), jnp.stack(vs_l),
            jnp.stack(cp_l), jnp.stack(cs_l), jnp.stack(sv_l))
```

```python
import functools

import jax
import jax.numpy as jnp
from jax import lax
from jax.experimental import pallas as pl
from jax.experimental.pallas import tpu as pltpu

D_MODEL = 2048
BATCH = 2
SEQ = 4096
DEPTH = 4
DEC_BATCH = 32
DEC_SEQ = 4
PAST_LEN = 16384
HEAD_DIM = 64
N_HEADS = D_MODEL // HEAD_DIM
N_KV_HEADS = N_HEADS // 8
ROT_DIM = HEAD_DIM // 4
ROPE_THETA = 500000.0
WINDOW = 128
BLOCK = 128
CONV_WIDTH = 3
SGU_WIDTH = D_MODEL
SGU_GROUPS = 8
SGU_GROUP_DIM = SGU_WIDTH // SGU_GROUPS
CHUNK = 128
D_FF = 5632
N_MIXERS = 3
RMS_EPS = 1e-6
LN_EPS = 1e-5
NEG_INF = -1e30

M_PROMPT = BATCH * SEQ
M_SAMPLE = DEC_BATCH * DEC_SEQ
M_ALL = M_PROMPT + M_SAMPLE
KV_DIM = N_KV_HEADS * HEAD_DIM
QKV_DIM = D_MODEL + 2 * KV_DIM

LANES = 128
SUBLANES = 8
BF16_SUBLANES = 16
MXU_COLS = 256
HEADS_PER_TILE = LANES // HEAD_DIM
Q_TILES = D_MODEL // LANES
TILES_PER_KV = Q_TILES // N_KV_HEADS
KEYS = 2 * BLOCK

TM = 640
N_ROW_TILES = M_ALL // TM
SAMPLE_ROW0 = M_PROMPT - (N_ROW_TILES - 1) * TM
TF = 512
CONV_TN = 512
VMEM_LIMIT = 58 * 1024 * 1024

assert M_ALL % TM == 0 and SAMPLE_ROW0 + M_SAMPLE == TM and M_SAMPLE == CHUNK and TM % CHUNK == 0
assert M_PROMPT % DEC_SEQ == 0 and SEQ % CHUNK == 0

F32 = jnp.float32
BF16 = jnp.bfloat16


def _rms(x, g):
    return x * lax.rsqrt(jnp.mean(x * x, axis=-1, keepdims=True) + RMS_EPS) * g


def _params(*sem):
    return pltpu.CompilerParams(dimension_semantics=sem, vmem_limit_bytes=VMEM_LIMIT)


def _wspec(lead, block, idx):
    return pl.BlockSpec((None,) * len(lead) + block, lambda *g: lead + idx(*g))


def _ffn_kernel(*refs, nj, n_cast, split_in, split_out):
    refs = list(refs)
    x_ref = refs.pop(0)
    xs_ref = refs.pop(0) if split_in else None
    gpre_ref, gpost_ref, wg_ref, wu_ref, wd_ref = refs[:5]
    cast_src = refs[5:5 + n_cast]
    refs = refs[5 + n_cast:]
    o_ref = refs.pop(0)
    os_ref = refs.pop(0) if split_out else None
    cast_dst = refs[:n_cast]
    h_ref, acc_ref = refs[n_cast:]
    j = pl.program_id(1)

    def read_x():
        x = x_ref[...]
        if xs_ref is not None:
            row = lax.broadcasted_iota(jnp.int32, (TM, 1), 0)
            tail = (pl.program_id(0) == N_ROW_TILES - 1) & (row >= SAMPLE_ROW0)
            top = jnp.zeros((SAMPLE_ROW0, D_MODEL), F32)
            x = jnp.where(tail, jnp.concatenate([top, xs_ref[...]], axis=0), x)
        return x

    @pl.when(j == 0)
    def _():
        h_ref[...] = _rms(read_x(), gpre_ref[...]).astype(BF16)
        acc_ref[...] = jnp.zeros_like(acc_ref)

    h = h_ref[...]
    g = jnp.dot(h, wg_ref[...], preferred_element_type=F32)
    u = jnp.dot(h, wu_ref[...], preferred_element_type=F32)
    a = (g * jax.nn.sigmoid(g) * u).astype(BF16)
    acc_ref[...] += jnp.dot(a, wd_ref[...], preferred_element_type=F32)
    for src, dst in zip(cast_src, cast_dst):
        dst[...] = src[...].astype(BF16)

    @pl.when(j == nj - 1)
    def _():
        y = read_x() + _rms(acc_ref[...], 0.5 * gpost_ref[...])
        o_ref[...] = y
        if os_ref is not None:
            os_ref[...] = y[SAMPLE_ROW0:]


def _cast_rows_per_step(rows, steps):
    rps = BF16_SUBLANES
    while rows % rps or rows // rps > steps:
        rps += BF16_SUBLANES
    return rps


def _ffn(x, g_pre, g_post, w_gu, w_down, casts=(), split_out=False):
    ni, nj = N_ROW_TILES, D_FF // TF
    row_tile = pl.BlockSpec((TM, D_MODEL), lambda i, j: (i, 0))
    sample_rows = pl.BlockSpec((M_SAMPLE, D_MODEL), lambda i, j: (0, 0))
    split_in = isinstance(x, tuple)
    x_args, x_specs = (list(x), [row_tile, sample_rows]) if split_in else ([x], [row_tile])
    if split_out:
        o_specs = [row_tile, sample_rows]
        o_shapes = [jax.ShapeDtypeStruct((M_PROMPT, D_MODEL), F32), jax.ShapeDtypeStruct((M_SAMPLE, D_MODEL), F32)]
    else:
        o_specs, o_shapes = [row_tile], [jax.ShapeDtypeStruct((M_ALL, D_MODEL), F32)]
    cast_in_specs, cast_out_specs, cast_shapes, cast_args = [], [], [], []
    for w, lead in casts:
        rows, cols = w.shape[-2:]
        rps = _cast_rows_per_step(rows, ni * nj)
        slab = lambda i, j, last=rows // rps - 1: (jnp.minimum(i * nj + j, last), 0)
        cast_in_specs.append(_wspec(lead, (rps, cols), slab))
        cast_out_specs.append(pl.BlockSpec((rps, cols), slab))
        cast_shapes.append(jax.ShapeDtypeStruct((rows, cols), BF16))
        cast_args.append(w)
    out = pl.pallas_call(
        functools.partial(_ffn_kernel, nj=nj, n_cast=len(casts), split_in=split_in, split_out=split_out),
        grid=(ni, nj),
        in_specs=x_specs + [
            pl.BlockSpec((1, D_MODEL), lambda i, j: (0, 0)),
            pl.BlockSpec((1, D_MODEL), lambda i, j: (0, 0)),
            pl.BlockSpec((D_MODEL, TF), lambda i, j: (0, j)),
            pl.BlockSpec((D_MODEL, TF), lambda i, j: (0, j + nj)),
            pl.BlockSpec((TF, D_MODEL), lambda i, j: (j, 0)),
        ] + cast_in_specs,
        out_specs=o_specs + cast_out_specs,
        out_shape=o_shapes + cast_shapes,
        scratch_shapes=[pltpu.VMEM((TM, D_MODEL), BF16), pltpu.VMEM((TM, D_MODEL), F32)],
        compiler_params=_params("arbitrary", "arbitrary"),
        name="ffn",
    )(*x_args, g_pre, g_post, w_gu, w_gu, w_down, *cast_args)
    n_x = len(o_specs)
    return (tuple(out[:n_x]) if split_out else out[0]), out[n_x:]


def _outproj_kernel(x_ref, a_ref, g_ref, w_ref, o_ref):
    y = jnp.dot(a_ref[...], w_ref[...], preferred_element_type=F32)
    o_ref[...] = x_ref[...] + _rms(y, g_ref[...])


def _outproj(x, a, g, w):
    return pl.pallas_call(
        _outproj_kernel,
        grid=(N_ROW_TILES,),
        in_specs=[
            pl.BlockSpec((TM, D_MODEL), lambda i: (i, 0)),
            pl.BlockSpec((TM, D_MODEL), lambda i: (i, 0)),
            pl.BlockSpec((1, D_MODEL), lambda i: (0, 0)),
            pl.BlockSpec((D_MODEL, D_MODEL), lambda i: (0, 0)),
        ],
        out_specs=pl.BlockSpec((TM, D_MODEL), lambda i: (i, 0)),
        out_shape=jax.ShapeDtypeStruct((M_ALL, D_MODEL), F32),
        compiler_params=_params("arbitrary"),
        name="outproj",
    )(x, a, g, w)


def _rope_tables():
    inv = jnp.power(jnp.float32(ROPE_THETA), -jnp.arange(0, ROT_DIM, 2, dtype=jnp.float32) / ROT_DIM)
    pos = jnp.concatenate([
        jnp.tile(jnp.arange(SEQ), BATCH),
        jnp.tile(PAST_LEN + jnp.arange(DEC_SEQ), DEC_BATCH),
    ])
    ang = pos.astype(jnp.float32)[:, None] * inv[None, :]
    c, s = jnp.cos(ang), jnp.sin(ang)
    pad = jnp.zeros((M_ALL, HEAD_DIM - ROT_DIM), F32)
    zero = jnp.zeros_like(s)
    cos_h = jnp.concatenate([c, c, pad + 1.0], axis=1)
    sa_h = jnp.concatenate([-s, zero, pad], axis=1)
    sb_h = jnp.concatenate([zero, s, pad], axis=1)
    rep = lambda t: jnp.tile(t, (1, HEADS_PER_TILE))
    return rep(cos_h), rep(sa_h), rep(sb_h)


def _attn_proj_kernel(x_ref, g_ref, w_ref, cos_ref, sa_ref, sb_ref, q_ref, kv_ref):
    h = _rms(x_ref[...], g_ref[...]).astype(BF16)
    cos, sa, sb = cos_ref[...], sa_ref[...], sb_ref[...]
    half = ROT_DIM // 2
    for c in range(QKV_DIM // MXU_COLS):
        y2 = jnp.dot(h, w_ref[:, c * MXU_COLS:(c + 1) * MXU_COLS], preferred_element_type=F32)
        for part in range(MXU_COLS // LANES):
            t = c * (MXU_COLS // LANES) + part
            y = y2[:, part * LANES:(part + 1) * LANES]
            if t < Q_TILES + KV_DIM // LANES:
                y = y * cos + pltpu.roll(y, LANES - half, axis=1) * sa + pltpu.roll(y, half, axis=1) * sb
            if t < Q_TILES:
                q_ref[:, t * LANES:(t + 1) * LANES] = (y * (HEAD_DIM ** -0.5)).astype(BF16)
            else:
                kv_ref[:, (t - Q_TILES) * LANES:(t - Q_TILES + 1) * LANES] = y


def _attn_proj(x, g, w, tables):
    row = lambda i: (i, 0)
    fixed = lambda i: (0, 0)
    return pl.pallas_call(
        _attn_proj_kernel,
        grid=(N_ROW_TILES,),
        in_specs=[
            pl.BlockSpec((TM, D_MODEL), row),
            pl.BlockSpec((1, D_MODEL), fixed),
            pl.BlockSpec((D_MODEL, QKV_DIM), fixed),
            pl.BlockSpec((TM, LANES), row),
            pl.BlockSpec((TM, LANES), row),
            pl.BlockSpec((TM, LANES), row),
        ],
        out_specs=[pl.BlockSpec((TM, D_MODEL), row), pl.BlockSpec((TM, 2 * KV_DIM), row)],
        out_shape=[jax.ShapeDtypeStruct((M_ALL, D_MODEL), BF16),
                   jax.ShapeDtypeStruct((M_ALL, 2 * KV_DIM), F32)],
        compiler_params=_params("arbitrary"),
        name="attn_proj",
    )(x, g, w, *tables)


def _both_halves(tile, first_half):
    lane = lax.broadcasted_iota(jnp.int32, tile.shape, 1)
    rolled = pltpu.roll(tile, HEAD_DIM, axis=1)
    keep = (lane < HEAD_DIM) if first_half else (lane >= HEAD_DIM)
    return jnp.where(keep, tile, rolled)


def _block_diag(rep):
    lane = lax.broadcasted_iota(jnp.int32, rep.shape, 1)
    top = jnp.where(lane < HEAD_DIM, rep, 0.0)
    bot = jnp.where(lane >= HEAD_DIM, rep, 0.0)
    return jnp.concatenate([top, bot], axis=0).astype(BF16)


def _kv_block_diag(k_all, v_all, h):
    lo = (h // HEADS_PER_TILE) * LANES
    first_half = h % HEADS_PER_TILE == 0
    return (_block_diag(_both_halves(k_all[:, lo:lo + LANES], first_half)),
            _block_diag(_both_halves(v_all[:, lo:lo + LANES], first_half)))


def _attend(q, k_bd, v_bd, mask, sinks):
    rows = mask.shape[0]
    s_all = lax.dot_general(q, k_bd, (((1,), (1,)), ((), ())), preferred_element_type=F32)
    ps, invs = [], []
    for t, (sink_a, sink_b) in enumerate(sinks):
        s = s_all[t * rows:(t + 1) * rows]
        sa = jnp.where(mask, s[:, :KEYS], NEG_INF)
        sb = jnp.where(mask, s[:, KEYS:], NEG_INF)
        ma = jnp.maximum(jnp.max(sa, axis=-1, keepdims=True), sink_a)
        mb = jnp.maximum(jnp.max(sb, axis=-1, keepdims=True), sink_b)
        pa = jnp.exp(sa - ma)
        pb = jnp.exp(sb - mb)
        da = jnp.sum(pa, axis=-1, keepdims=True) + jnp.exp(sink_a - ma)
        db = jnp.sum(pb, axis=-1, keepdims=True) + jnp.exp(sink_b - mb)
        ps.append(jnp.concatenate([pa, pb], axis=1).astype(BF16))
        lane = lax.broadcasted_iota(jnp.int32, (rows, LANES), 1)
        invs.append(1.0 / jnp.where(lane < HEAD_DIM, da, db))
    o = jnp.dot(jnp.concatenate(ps, axis=0), v_bd, preferred_element_type=F32)
    return o * jnp.concatenate(invs, axis=0)


def _window_mask(rows, q_index, min_key):
    kj = lax.broadcasted_iota(jnp.int32, (rows, KEYS), 1)
    return (kj >= q_index) & (kj <= q_index + WINDOW) & (kj >= min_key)


def _attn_prompt_kernel(sink_ref, q_ref, kvc_ref, kvp_ref, o_ref):
    min_key = jnp.where(pl.program_id(1) == 0, BLOCK, 0)
    kv = jnp.concatenate([kvp_ref[...], kvc_ref[...]], axis=0)
    qi = lax.broadcasted_iota(jnp.int32, (BLOCK, KEYS), 0)
    mask = _window_mask(BLOCK, qi, min_key)
    for h in range(N_KV_HEADS):
        k_bd, v_bd = _kv_block_diag(kv[:, :KV_DIM], kv[:, KV_DIM:], h)
        tiles = range(h * TILES_PER_KV, (h + 1) * TILES_PER_KV)
        q = jnp.concatenate([q_ref[:, t * LANES:(t + 1) * LANES] for t in tiles], axis=0)
        sinks = [(sink_ref[HEADS_PER_TILE * t], sink_ref[HEADS_PER_TILE * t + 1]) for t in tiles]
        o = _attend(q, k_bd, v_bd, mask, sinks).astype(BF16)
        for n, t in enumerate(tiles):
            o_ref[:, t * LANES:(t + 1) * LANES] = o[n * BLOCK:(n + 1) * BLOCK]


def _attn_prompt(q, kv, sinks):
    nb = SEQ // BLOCK
    return pl.pallas_call(
        _attn_prompt_kernel,
        grid=(BATCH, nb),
        in_specs=[
            pl.BlockSpec(memory_space=pltpu.SMEM),
            pl.BlockSpec((BLOCK, D_MODEL), lambda b, n: (b * nb + n, 0)),
            pl.BlockSpec((BLOCK, 2 * KV_DIM), lambda b, n: (b * nb + n, 0)),
            pl.BlockSpec((BLOCK, 2 * KV_DIM), lambda b, n: (b * nb + jnp.maximum(n - 1, 0), 0)),
        ],
        out_specs=pl.BlockSpec((BLOCK, D_MODEL), lambda b, n: (b * nb + n, 0)),
        out_shape=jax.ShapeDtypeStruct((M_ALL, D_MODEL), BF16),
        compiler_params=_params("arbitrary", "arbitrary"),
        name="attn_prompt",
    )(sinks, q, kv, kv)


SAMPLE_Q_ROWS = Q_TILES * DEC_SEQ
ROWS_PER_KV = TILES_PER_KV * DEC_SEQ


def _attn_sample_kernel(sink_ref, q_ref, ck_ref, cv_ref, kn_ref, vn_ref, o_ref, kbuf, vbuf):
    pad = jnp.zeros((KEYS - WINDOW, KV_DIM), F32)
    kbuf[WINDOW:, :] = pad
    vbuf[WINDOW:, :] = pad
    kbuf[:WINDOW, :] = ck_ref[0]
    vbuf[:WINDOW, :] = cv_ref[0]
    kbuf[WINDOW:WINDOW + DEC_SEQ, :] = kn_ref[0]
    vbuf[WINDOW:WINDOW + DEC_SEQ, :] = vn_ref[0]
    k_all = kbuf[...]
    v_all = vbuf[...]
    row = lax.broadcasted_iota(jnp.int32, (ROWS_PER_KV, KEYS), 0)
    mask = _window_mask(ROWS_PER_KV, row % DEC_SEQ, 0)
    tile_of_row = lax.broadcasted_iota(jnp.int32, (ROWS_PER_KV, 1), 0) // DEC_SEQ
    for h in range(N_KV_HEADS):
        k_bd, v_bd = _kv_block_diag(k_all, v_all, h)
        sink_a = jnp.zeros((ROWS_PER_KV, 1), F32)
        sink_b = jnp.zeros((ROWS_PER_KV, 1), F32)
        for t in range(TILES_PER_KV):
            tile = h * TILES_PER_KV + t
            sink_a = jnp.where(tile_of_row == t, sink_ref[HEADS_PER_TILE * tile], sink_a)
            sink_b = jnp.where(tile_of_row == t, sink_ref[HEADS_PER_TILE * tile + 1], sink_b)
        rows = slice(h * ROWS_PER_KV, (h + 1) * ROWS_PER_KV)
        o_ref[0, rows, :] = _attend(q_ref[0, rows, :], k_bd, v_bd, mask, [(sink_a, sink_b)]).astype(BF16)


def _attn_sample(q_s, cache_k, cache_v, k_new, v_new, sinks):
    per_b = lambda b: (b, 0, 0)
    return pl.pallas_call(
        _attn_sample_kernel,
        grid=(DEC_BATCH,),
        in_specs=[
            pl.BlockSpec(memory_space=pltpu.SMEM),
            pl.BlockSpec((1, SAMPLE_Q_ROWS, LANES), per_b),
            pl.BlockSpec((1, WINDOW, KV_DIM), per_b),
            pl.BlockSpec((1, WINDOW, KV_DIM), per_b),
            pl.BlockSpec((1, DEC_SEQ, KV_DIM), per_b),
            pl.BlockSpec((1, DEC_SEQ, KV_DIM), per_b),
        ],
        out_specs=pl.BlockSpec((1, SAMPLE_Q_ROWS, LANES), per_b),
        out_shape=jax.ShapeDtypeStruct((DEC_BATCH, SAMPLE_Q_ROWS, LANES), BF16),
        scratch_shapes=[pltpu.VMEM((KEYS, KV_DIM), F32), pltpu.VMEM((KEYS, KV_DIM), F32)],
        compiler_params=_params("arbitrary"),
        name="attn_sample",
    )(sinks, q_s, cache_k, cache_v, k_new, v_new)


def _prompt_tail_slices():
    out = []
    for b in range(BATCH):
        end = (b + 1) * SEQ
        tile = (end - 1) // TM
        out.append((tile, end - tile * TM - SUBLANES))
    return out


def _conv_mixer_kernel(x_ref, g_ref, wb_ref, wc_ref, wh_ref, cw_ref, p1_ref, p2_ref,
                       a_ref, zp_ref, zs_ref, h_ref, carry_ref):
    i, j = pl.program_id(0), pl.program_id(1)

    @pl.when(j == 0)
    def _():
        h_ref[...] = _rms(x_ref[...], g_ref[...]).astype(BF16)

    @pl.when(i == 0)
    def _():
        carry_ref[j] = jnp.zeros(carry_ref.shape[1:], F32)

    h = h_ref[...]
    b = jnp.dot(h, wb_ref[...], preferred_element_type=F32)
    c = jnp.dot(h, wc_ref[...], preferred_element_type=F32)
    hin = jnp.dot(h, wh_ref[...], preferred_element_type=F32)
    z = c * hin
    prev = carry_ref[j]
    carry_ref[j] = z[TM - SUBLANES:]

    row = lax.broadcasted_iota(jnp.int32, (TM, 1), 0)
    grow = row + i * TM
    is_sample = grow >= M_PROMPT
    t = jnp.where(is_sample, grow % DEC_SEQ, grow % SEQ)
    top = jnp.zeros((SAMPLE_ROW0, z.shape[1]), F32)
    pre1 = jnp.where(is_sample, jnp.concatenate([top, p1_ref[...]], axis=0), 0.0)
    pre2 = jnp.where(is_sample, jnp.concatenate([top, p2_ref[...]], axis=0), 0.0)
    last = prev[SUBLANES - 1:SUBLANES]
    zm1 = jnp.where(row == 0, last, pltpu.roll(z, 1, axis=0))
    zm2 = jnp.where(row == 0, prev[SUBLANES - 2:SUBLANES - 1],
                    jnp.where(row == 1, last, pltpu.roll(z, 2, axis=0)))
    zm1 = jnp.where(t < 1, pre1, zm1)
    zm2 = jnp.where(t < 2, pre2, zm2)
    y = cw_ref[0:1, :] * zm2 + cw_ref[1:2, :] * zm1 + cw_ref[2:3, :] * z
    a_ref[...] = (b * y).astype(BF16)

    zs_ref[...] = z[SAMPLE_ROW0:]
    tails = _prompt_tail_slices()
    zp = z[tails[0][1]:tails[0][1] + SUBLANES]
    for tile, lo in tails[1:]:
        zp = jnp.where(i == tile, z[lo:lo + SUBLANES], zp)
    zp_ref[...] = zp


def _conv_mixer(x, g, w_in, w_conv, p1, p2):
    nj = D_MODEL // CONV_TN
    col = lambda k: (lambda i, j: (0, j + k * nj))
    return pl.pallas_call(
        _conv_mixer_kernel,
        grid=(N_ROW_TILES, nj),
        in_specs=[
            pl.BlockSpec((TM, D_MODEL), lambda i, j: (i, 0)),
            pl.BlockSpec((1, D_MODEL), lambda i, j: (0, 0)),
            pl.BlockSpec((D_MODEL, CONV_TN), col(0)),
            pl.BlockSpec((D_MODEL, CONV_TN), col(1)),
            pl.BlockSpec((D_MODEL, CONV_TN), col(2)),
            pl.BlockSpec((CONV_WIDTH, CONV_TN), lambda i, j: (0, j)),
            pl.BlockSpec((M_SAMPLE, CONV_TN), lambda i, j: (0, j)),
            pl.BlockSpec((M_SAMPLE, CONV_TN), lambda i, j: (0, j)),
        ],
        out_specs=[
            pl.BlockSpec((TM, CONV_TN), lambda i, j: (i, j)),
            pl.BlockSpec((SUBLANES, CONV_TN), lambda i, j: (i, j)),
            pl.BlockSpec((M_SAMPLE, CONV_TN), lambda i, j: (0, jnp.where(i == N_ROW_TILES - 1, j, 0))),
        ],
        out_shape=[
            jax.ShapeDtypeStruct((M_ALL, D_MODEL), BF16),
            jax.ShapeDtypeStruct((N_ROW_TILES * SUBLANES, D_MODEL), F32),
            jax.ShapeDtypeStruct((M_SAMPLE, D_MODEL), F32),
        ],
        scratch_shapes=[pltpu.VMEM((TM, D_MODEL), BF16), pltpu.VMEM((nj, SUBLANES, CONV_TN), F32)],
        compiler_params=_params("arbitrary", "arbitrary"),
        name="conv_mixer",
    )(x, g, w_in, w_in, w_in, w_conv, p1, p2)


def _sgu_mixer_kernel(x_ref, g_ref, w_ref, lg_ref, lb_ref, ws_ref, bst_ref, wss_ref, bsts_ref,
                      a_ref, vln_ref, h_ref, mixed_ref):
    i, j = pl.program_id(0), pl.program_id(1)
    n_chunks = TM // CHUNK

    @pl.when(j == 0)
    def _():
        h = _rms(x_ref[...], g_ref[...]).astype(BF16)
        h_ref[...] = h
        v = jax.nn.gelu(jnp.dot(h, w_ref[...], preferred_element_type=F32))
        mu = jnp.mean(v, axis=-1, keepdims=True)
        xc = v - mu
        vln = xc * lax.rsqrt(jnp.mean(xc * xc, axis=-1, keepdims=True) + LN_EPS) * lg_ref[...] + lb_ref[...]
        vln_ref[...] = vln[SAMPLE_ROW0:]
        vb = vln.astype(BF16)
        r = lax.broadcasted_iota(jnp.int32, (CHUNK, CHUNK), 0)
        c = lax.broadcasted_iota(jnp.int32, (CHUNK, CHUNK), 1)
        causal = r >= c
        last_tile = i == N_ROW_TILES - 1
        for grp in range(SGU_GROUPS):
            cols = slice(grp * SGU_GROUP_DIM, (grp + 1) * SGU_GROUP_DIM)
            w_p = jnp.where(causal, ws_ref[grp], 0.0)
            w_s = jnp.where(last_tile, jnp.where(causal, wss_ref[grp], 0.0), w_p)
            b_p = bst_ref[:, grp:grp + 1]
            b_s = jnp.where(last_tile, bsts_ref[:, grp:grp + 1], b_p)
            for ch in range(n_chunks):
                rows = slice(ch * CHUNK, (ch + 1) * CHUNK)
                sample_chunk = ch == n_chunks - 1
                w = (w_s if sample_chunk else w_p).astype(BF16)
                bias = b_s if sample_chunk else b_p
                mixed_ref[rows, cols] = jnp.dot(w, vb[rows, cols], preferred_element_type=F32) + bias

    @pl.when(j == 1)
    def _():
        u = jax.nn.gelu(jnp.dot(h_ref[...], w_ref[...], preferred_element_type=F32))
        a_ref[...] = (u * mixed_ref[...]).astype(BF16)


def _sgu_mixer(x, g, w_in, ln_g, ln_b, ws, bst, ws_s, bst_s):
    fixed2 = lambda i, j: (0, 0)
    return pl.pallas_call(
        _sgu_mixer_kernel,
        grid=(N_ROW_TILES, 2),
        in_specs=[
            pl.BlockSpec((TM, D_MODEL), lambda i, j: (i, 0)),
            pl.BlockSpec((1, D_MODEL), fixed2),
            pl.BlockSpec((D_MODEL, SGU_WIDTH), lambda i, j: (0, 1 - j)),
            pl.BlockSpec((1, SGU_WIDTH), fixed2),
            pl.BlockSpec((1, SGU_WIDTH), fixed2),
            pl.BlockSpec((SGU_GROUPS, CHUNK, CHUNK), lambda i, j: (0, 0, 0)),
            pl.BlockSpec((CHUNK, SGU_GROUPS), fixed2),
            pl.BlockSpec((SGU_GROUPS, CHUNK, CHUNK), lambda i, j: (0, 0, 0)),
            pl.BlockSpec((CHUNK, SGU_GROUPS), fixed2),
        ],
        out_specs=[
            pl.BlockSpec((TM, SGU_WIDTH), lambda i, j: (i, 0)),
            pl.BlockSpec((M_SAMPLE, SGU_WIDTH), fixed2),
        ],
        out_shape=[
            jax.ShapeDtypeStruct((M_ALL, SGU_WIDTH), BF16),
            jax.ShapeDtypeStruct((M_SAMPLE, SGU_WIDTH), F32),
        ],
        scratch_shapes=[pltpu.VMEM((TM, D_MODEL), BF16), pltpu.VMEM((TM, SGU_WIDTH), F32)],
        compiler_params=_params("arbitrary", "arbitrary"),
        name="sgu_mixer",
    )(x, g, w_in, ln_g, ln_b, ws, bst, ws_s, bst_s)


def _attention_layer(x, g_pre, w_qkv, sinks, cache_k, cache_v, tables):
    q, kv = _attn_proj(x, g_pre, w_qkv, tables)
    o_p = _attn_prompt(q, kv, sinks)

    q_s = q[M_PROMPT:].reshape(DEC_BATCH, DEC_SEQ, Q_TILES, LANES).transpose(0, 2, 1, 3)
    q_s = q_s.reshape(DEC_BATCH, SAMPLE_Q_ROWS, LANES)
    kv_s = kv[M_PROMPT:].reshape(DEC_BATCH, DEC_SEQ, 2 * KV_DIM)
    k_new, v_new = kv_s[..., :KV_DIM], kv_s[..., KV_DIM:]
    ck = cache_k.reshape(DEC_BATCH, WINDOW, KV_DIM)
    cv = cache_v.reshape(DEC_BATCH, WINDOW, KV_DIM)
    o_s = _attn_sample(q_s, ck, cv, k_new, v_new, sinks)
    o_s = o_s.reshape(DEC_BATCH, Q_TILES, DEC_SEQ, LANES).transpose(0, 2, 1, 3).reshape(M_SAMPLE, D_MODEL)

    kv_p = kv[:M_PROMPT].reshape(BATCH, SEQ, 2 * KV_DIM)[:, SEQ - WINDOW:]
    kv_shape = (BATCH, WINDOW, N_KV_HEADS, HEAD_DIM)
    new_kp = kv_p[..., :KV_DIM].reshape(kv_shape)
    new_vp = kv_p[..., KV_DIM:].reshape(kv_shape)
    kv_shape = (DEC_BATCH, WINDOW, N_KV_HEADS, HEAD_DIM)
    new_ks = jnp.concatenate([ck[:, DEC_SEQ:], k_new], axis=1).reshape(kv_shape)
    new_vs = jnp.concatenate([cv[:, DEC_SEQ:], v_new], axis=1).reshape(kv_shape)
    a = lax.dynamic_update_slice(o_p, o_s, (M_PROMPT, 0))
    return a, (new_kp, new_vp, new_ks, new_vs)


def _conv_layer(x, g_pre, w_in, w_conv, state):
    zeros = jnp.zeros((DEC_BATCH, DEC_SEQ - 1, D_MODEL), F32)
    p1 = jnp.concatenate([state[:, 1:2], zeros], axis=1).reshape(M_SAMPLE, D_MODEL)
    p2 = jnp.concatenate([state[:, 0:1], state[:, 1:2], zeros[:, 1:]], axis=1).reshape(M_SAMPLE, D_MODEL)
    a, z_p, z_s = _conv_mixer(x, g_pre, w_in, w_conv, p1, p2)
    keep = CONV_WIDTH - 1
    new_cp = jnp.stack([z_p[(tile + 1) * SUBLANES - keep:(tile + 1) * SUBLANES]
                        for tile, _ in _prompt_tail_slices()])
    new_cs = z_s.reshape(DEC_BATCH, DEC_SEQ, D_MODEL)[:, DEC_SEQ - keep:]
    return a, (new_cp, new_cs)


def _sgu_layer(x, g_pre, w_in, ln_g, ln_b, w_s, b_s):
    eye = jnp.eye(DEC_BATCH, dtype=F32)
    ws_s = jnp.einsum("ab,gts->gatbs", eye, w_s[:, :DEC_SEQ, :DEC_SEQ]).reshape(SGU_GROUPS, M_SAMPLE, M_SAMPLE)
    bst_s = jnp.tile(b_s[:, :DEC_SEQ].T, (DEC_BATCH, 1))
    a, vln_s = _sgu_mixer(x, g_pre, w_in, ln_g, ln_b, w_s, b_s.T, ws_s, bst_s)
    return a, vln_s.reshape(DEC_BATCH, DEC_SEQ, SGU_WIDTH)


def kernel(x_prompt, x_sample, cache_k, cache_v, state_conv, norm_g, ffn_w_gu, ffn_w_down,
           attn_w_qkv, attn_w_o, attn_sinks, conv_w_in, conv_w, conv_w_out,
           sgu_w_in, sgu_ln_g, sgu_ln_b, sgu_w_s, sgu_b_s, sgu_w_out):
    x = (x_prompt.reshape(M_PROMPT, D_MODEL), x_sample.reshape(M_SAMPLE, D_MODEL))
    tables = _rope_tables()
    gain = lambda i, k: norm_g[i, k].reshape(1, D_MODEL)
    mixer_weights = ((attn_w_qkv, attn_w_o), (conv_w_in, conv_w_out), (sgu_w_in, sgu_w_out))
    w_gu, w_down = ffn_w_gu[0, 0].astype(BF16), ffn_w_down[0, 0].astype(BF16)
    kp_l, vp_l, ks_l, vs_l, cp_l, cs_l, sv_l = [], [], [], [], [], [], []
    for i in range(DEPTH):
        kind, j = i % N_MIXERS, i // N_MIXERS
        w_in_f32, w_out_f32 = mixer_weights[kind]
        x, (w_in, w_out, w_gu, w_down) = _ffn(
            x, gain(i, 0), gain(i, 1), w_gu, w_down,
            casts=((w_in_f32, (j,)), (w_out_f32, (j,)), (ffn_w_gu, (i, 1)), (ffn_w_down, (i, 1))))
        if kind == 0:
            a, (kp, vp, ks, vs) = _attention_layer(x, gain(i, 2), w_in, attn_sinks[j],
                                                   cache_k[j], cache_v[j], tables)
            kp_l.append(kp); vp_l.append(vp); ks_l.append(ks); vs_l.append(vs)
        elif kind == 1:
            a, (cp, cs) = _conv_layer(x, gain(i, 2), w_in, conv_w[j], state_conv[j])
            cp_l.append(cp); cs_l.append(cs)
        else:
            a, sv = _sgu_layer(x, gain(i, 2), w_in, sgu_ln_g[j].reshape(1, SGU_WIDTH),
                               sgu_ln_b[j].reshape(1, SGU_WIDTH), sgu_w_s[j], sgu_b_s[j])
            sv_l.append(sv)
        x = _outproj(x, a, gain(i, 3), w_out)
        last = i + 1 == DEPTH
        casts = () if last else ((ffn_w_gu, (i + 1, 0)), (ffn_w_down, (i + 1, 0)))
        x, next_ffn = _ffn(x, gain(i, 4), gain(i, 5), w_gu, w_down, casts=casts, split_out=last)
        if next_ffn:
            w_gu, w_down = next_ffn
    y_prompt = x[0].reshape(BATCH, SEQ, D_MODEL)
    y_sample = x[1].reshape(DEC_BATCH, DEC_SEQ, D_MODEL)
    return (y_prompt, y_sample, jnp.stack(kp_l), jnp.stack(vp_l), jnp.stack(ks_l), jnp.stack(vs_l),
            jnp.stack(cp_l), jnp.stack(cs_l), jnp.stack(sv_l))
```

```python
import functools

import jax
import jax.numpy as jnp
from jax import lax
from jax.experimental import pallas as pl
from jax.experimental.pallas import tpu as pltpu

D_MODEL = 2048
BATCH = 2
SEQ = 4096
DEPTH = 4
DEC_BATCH = 32
DEC_SEQ = 4
PAST_LEN = 16384
HEAD_DIM = 64
N_HEADS = D_MODEL // HEAD_DIM
N_KV_HEADS = N_HEADS // 8
ROT_DIM = HEAD_DIM // 4
ROPE_THETA = 500000.0
WINDOW = 128
BLOCK = 128
CONV_WIDTH = 3
SGU_WIDTH = D_MODEL
SGU_GROUPS = 8
SGU_GROUP_DIM = SGU_WIDTH // SGU_GROUPS
CHUNK = 128
D_FF = 5632
N_MIXERS = 3
RMS_EPS = 1e-6
LN_EPS = 1e-5
NEG_INF = -1e30

M_PROMPT = BATCH * SEQ
M_SAMPLE = DEC_BATCH * DEC_SEQ
M_ALL = M_PROMPT + M_SAMPLE
KV_DIM = N_KV_HEADS * HEAD_DIM
QKV_DIM = D_MODEL + 2 * KV_DIM

LANES = 128
SUBLANES = 8
BF16_SUBLANES = 16
MXU_COLS = 256
HEADS_PER_TILE = LANES // HEAD_DIM
Q_TILES = D_MODEL // LANES
TILES_PER_KV = Q_TILES // N_KV_HEADS
KEYS = 2 * BLOCK

TM = 640
N_ROW_TILES = M_ALL // TM
SAMPLE_ROW0 = M_PROMPT - (N_ROW_TILES - 1) * TM
TF = 512
ROW_CHUNK = 16
CONV_TN = 512
VMEM_LIMIT = 58 * 1024 * 1024

assert M_ALL % TM == 0 and SAMPLE_ROW0 + M_SAMPLE == TM and M_SAMPLE == CHUNK and TM % CHUNK == 0
assert M_PROMPT % DEC_SEQ == 0 and SEQ % CHUNK == 0
assert TM % ROW_CHUNK == 0 and SAMPLE_ROW0 % ROW_CHUNK == 0 and ROW_CHUNK % BF16_SUBLANES == 0

F32 = jnp.float32
BF16 = jnp.bfloat16


def _rms(x, g):
    return x * lax.rsqrt(jnp.mean(x * x, axis=-1, keepdims=True) + RMS_EPS) * g


def _params(*sem):
    return pltpu.CompilerParams(dimension_semantics=sem, vmem_limit_bytes=VMEM_LIMIT)


def _wspec(lead, block, idx):
    return pl.BlockSpec((None,) * len(lead) + block, lambda *g: lead + idx(*g))


def _ffn_kernel(*refs, nj, n_cast, split_in, split_out):
    refs = list(refs)
    x_ref = refs.pop(0)
    xs_ref = refs.pop(0) if split_in else None
    gpre_ref, gpost_ref, wg_ref, wu_ref, wd_ref = refs[:5]
    cast_src = refs[5:5 + n_cast]
    refs = refs[5 + n_cast:]
    o_ref = refs.pop(0)
    os_ref = refs.pop(0) if split_out else None
    cast_dst = refs[:n_cast]
    h_ref, acc_ref = refs[n_cast:]
    i, j = pl.program_id(0), pl.program_id(1)
    row_chunks = [slice(c * ROW_CHUNK, (c + 1) * ROW_CHUNK) for c in range(TM // ROW_CHUNK)]

    def sample_part(rows):
        return slice(rows.start - SAMPLE_ROW0, rows.stop - SAMPLE_ROW0)

    def read_x(rows):
        x = x_ref[rows, :]
        if xs_ref is not None and rows.start >= SAMPLE_ROW0:
            x = jnp.where(i == N_ROW_TILES - 1, xs_ref[sample_part(rows), :], x)
        return x

    @pl.when((i == 0) & (j == 0))
    def _():
        acc_ref[...] = jnp.zeros_like(acc_ref)

    @pl.when(j == 0)
    def _():
        for rows in row_chunks:
            h_ref[rows, :] = _rms(read_x(rows), gpre_ref[...]).astype(BF16)

    h = h_ref[...]
    g = jnp.dot(h, wg_ref[...], preferred_element_type=F32)
    for src, dst in zip(cast_src, cast_dst):
        dst[...] = src[...].astype(BF16)
    u = jnp.dot(h, wu_ref[...], preferred_element_type=F32)
    a = (g * jax.nn.sigmoid(g) * u).astype(BF16)
    d = jnp.dot(a, wd_ref[...], preferred_element_type=F32)
    acc_ref[...] = jnp.where(j == 0, 0.0, acc_ref[...]) + d

    @pl.when(j == nj - 1)
    def _():
        gain = 0.5 * gpost_ref[...]
        for rows in row_chunks:
            y = read_x(rows) + _rms(acc_ref[rows, :], gain)
            o_ref[rows, :] = y
            if os_ref is not None and rows.start >= SAMPLE_ROW0:
                os_ref[sample_part(rows), :] = y


def _cast_rows_per_step(rows, steps):
    rps = BF16_SUBLANES
    while rows % rps or rows // rps > steps:
        rps += BF16_SUBLANES
    return rps


def _ffn(x, g_pre, g_post, w_gu, w_down, casts=(), split_out=False):
    ni, nj = N_ROW_TILES, D_FF // TF
    row_tile = pl.BlockSpec((TM, D_MODEL), lambda i, j: (i, 0))
    sample_rows = pl.BlockSpec((M_SAMPLE, D_MODEL), lambda i, j: (0, 0))
    split_in = isinstance(x, tuple)
    x_args, x_specs = (list(x), [row_tile, sample_rows]) if split_in else ([x], [row_tile])
    if split_out:
        o_specs = [row_tile, sample_rows]
        o_shapes = [jax.ShapeDtypeStruct((M_PROMPT, D_MODEL), F32), jax.ShapeDtypeStruct((M_SAMPLE, D_MODEL), F32)]
    else:
        o_specs, o_shapes = [row_tile], [jax.ShapeDtypeStruct((M_ALL, D_MODEL), F32)]
    cast_in_specs, cast_out_specs, cast_shapes, cast_args = [], [], [], []
    for w, lead in casts:
        rows, cols = w.shape[-2:]
        rps = _cast_rows_per_step(rows, ni * nj)
        slab = lambda i, j, last=rows // rps - 1: (jnp.minimum(i * nj + j, last), 0)
        cast_in_specs.append(_wspec(lead, (rps, cols), slab))
        cast_out_specs.append(pl.BlockSpec((rps, cols), slab))
        cast_shapes.append(jax.ShapeDtypeStruct((rows, cols), BF16))
        cast_args.append(w)
    out = pl.pallas_call(
        functools.partial(_ffn_kernel, nj=nj, n_cast=len(casts), split_in=split_in, split_out=split_out),
        grid=(ni, nj),
        in_specs=x_specs + [
            pl.BlockSpec((1, D_MODEL), lambda i, j: (0, 0)),
            pl.BlockSpec((1, D_MODEL), lambda i, j: (0, 0)),
            pl.BlockSpec((D_MODEL, TF), lambda i, j: (0, j)),
            pl.BlockSpec((D_MODEL, TF), lambda i, j: (0, j + nj)),
            pl.BlockSpec((TF, D_MODEL), lambda i, j: (j, 0)),
        ] + cast_in_specs,
        out_specs=o_specs + cast_out_specs,
        out_shape=o_shapes + cast_shapes,
        scratch_shapes=[pltpu.VMEM((TM, D_MODEL), BF16), pltpu.VMEM((TM, D_MODEL), F32)],
        compiler_params=_params("arbitrary", "arbitrary"),
        name="ffn",
    )(*x_args, g_pre, g_post, w_gu, w_gu, w_down, *cast_args)
    n_x = len(o_specs)
    return (tuple(out[:n_x]) if split_out else out[0]), out[n_x:]


def _outproj_kernel(*refs, split_a):
    x_ref, a_ref = refs[:2]
    as_ref = refs[2] if split_a else None
    g_ref, w_ref, o_ref, y_ref = refs[-4:]
    a = a_ref[...]
    if as_ref is not None:
        tail = jnp.where(pl.program_id(0) == N_ROW_TILES - 1, as_ref[...], a_ref[SAMPLE_ROW0:, :])
        a = jnp.concatenate([a_ref[:SAMPLE_ROW0, :], tail], axis=0)
    y_ref[...] = jnp.dot(a, w_ref[...], preferred_element_type=F32)
    for c in range(TM // ROW_CHUNK):
        rows = slice(c * ROW_CHUNK, (c + 1) * ROW_CHUNK)
        o_ref[rows, :] = x_ref[rows, :] + _rms(y_ref[rows, :], g_ref[...])


def _outproj(x, a, g, w):
    row_tile = pl.BlockSpec((TM, D_MODEL), lambda i: (i, 0))
    split_a = isinstance(a, tuple)
    a_args = list(a) if split_a else [a]
    a_specs = [row_tile] + ([pl.BlockSpec((M_SAMPLE, D_MODEL), lambda i: (0, 0))] if split_a else [])
    return pl.pallas_call(
        functools.partial(_outproj_kernel, split_a=split_a),
        grid=(N_ROW_TILES,),
        in_specs=[row_tile] + a_specs + [
            pl.BlockSpec((1, D_MODEL), lambda i: (0, 0)),
            pl.BlockSpec((D_MODEL, D_MODEL), lambda i: (0, 0)),
        ],
        out_specs=row_tile,
        out_shape=jax.ShapeDtypeStruct((M_ALL, D_MODEL), F32),
        scratch_shapes=[pltpu.VMEM((TM, D_MODEL), F32)],
        compiler_params=_params("arbitrary"),
        name="outproj",
    )(x, *a_args, g, w)


def _rope_tables():
    inv = jnp.power(jnp.float32(ROPE_THETA), -jnp.arange(0, ROT_DIM, 2, dtype=jnp.float32) / ROT_DIM)
    pos = jnp.concatenate([
        jnp.tile(jnp.arange(SEQ), BATCH),
        jnp.tile(PAST_LEN + jnp.arange(DEC_SEQ), DEC_BATCH),
    ])
    ang = pos.astype(jnp.float32)[:, None] * inv[None, :]
    c, s = jnp.cos(ang), jnp.sin(ang)
    pad = jnp.zeros((M_ALL, HEAD_DIM - ROT_DIM), F32)
    zero = jnp.zeros_like(s)
    cos_h = jnp.concatenate([c, c, pad + 1.0], axis=1)
    sa_h = jnp.concatenate([-s, zero, pad], axis=1)
    sb_h = jnp.concatenate([zero, s, pad], axis=1)
    rep = lambda t: jnp.tile(t, (1, HEADS_PER_TILE))
    return rep(cos_h), rep(sa_h), rep(sb_h)


def _attn_proj_kernel(x_ref, g_ref, w_ref, cos_ref, sa_ref, sb_ref, q_ref, kv_ref):
    h = _rms(x_ref[...], g_ref[...]).astype(BF16)
    cos, sa, sb = cos_ref[...], sa_ref[...], sb_ref[...]
    half = ROT_DIM // 2
    for c in range(QKV_DIM // MXU_COLS):
        y2 = jnp.dot(h, w_ref[:, c * MXU_COLS:(c + 1) * MXU_COLS], preferred_element_type=F32)
        for part in range(MXU_COLS // LANES):
            t = c * (MXU_COLS // LANES) + part
            y = y2[:, part * LANES:(part + 1) * LANES]
            if t < Q_TILES + KV_DIM // LANES:
                y = y * cos + pltpu.roll(y, LANES - half, axis=1) * sa + pltpu.roll(y, half, axis=1) * sb
            if t < Q_TILES:
                q_ref[:, t * LANES:(t + 1) * LANES] = (y * (HEAD_DIM ** -0.5)).astype(BF16)
            else:
                kv_ref[:, (t - Q_TILES) * LANES:(t - Q_TILES + 1) * LANES] = y


def _attn_proj(x, g, w, tables):
    row = lambda i: (i, 0)
    fixed = lambda i: (0, 0)
    return pl.pallas_call(
        _attn_proj_kernel,
        grid=(N_ROW_TILES,),
        in_specs=[
            pl.BlockSpec((TM, D_MODEL), row),
            pl.BlockSpec((1, D_MODEL), fixed),
            pl.BlockSpec((D_MODEL, QKV_DIM), fixed),
            pl.BlockSpec((TM, LANES), row),
            pl.BlockSpec((TM, LANES), row),
            pl.BlockSpec((TM, LANES), row),
        ],
        out_specs=[pl.BlockSpec((TM, D_MODEL), row), pl.BlockSpec((TM, 2 * KV_DIM), row)],
        out_shape=[jax.ShapeDtypeStruct((M_ALL, D_MODEL), BF16),
                   jax.ShapeDtypeStruct((M_ALL, 2 * KV_DIM), F32)],
        compiler_params=_params("arbitrary"),
        name="attn_proj",
    )(x, g, w, *tables)


def _both_halves(tile, first_half):
    lane = lax.broadcasted_iota(jnp.int32, tile.shape, 1)
    rolled = pltpu.roll(tile, HEAD_DIM, axis=1)
    keep = (lane < HEAD_DIM) if first_half else (lane >= HEAD_DIM)
    return jnp.where(keep, tile, rolled)


def _block_diag(rep):
    lane = lax.broadcasted_iota(jnp.int32, rep.shape, 1)
    top = jnp.where(lane < HEAD_DIM, rep, 0.0)
    bot = jnp.where(lane >= HEAD_DIM, rep, 0.0)
    return jnp.concatenate([top, bot], axis=0).astype(BF16)


def _kv_block_diag(k_all, v_all, h):
    lo = (h // HEADS_PER_TILE) * LANES
    first_half = h % HEADS_PER_TILE == 0
    return (_block_diag(_both_halves(k_all[:, lo:lo + LANES], first_half)),
            _block_diag(_both_halves(v_all[:, lo:lo + LANES], first_half)))


def _attend(q, k_bd, v_bd, mask, sinks):
    rows = mask.shape[0]
    s_all = lax.dot_general(q, k_bd, (((1,), (1,)), ((), ())), preferred_element_type=F32)
    ps, invs = [], []
    for t, (sink_a, sink_b) in enumerate(sinks):
        s = s_all[t * rows:(t + 1) * rows]
        sa = jnp.where(mask, s[:, :KEYS], NEG_INF)
        sb = jnp.where(mask, s[:, KEYS:], NEG_INF)
        ma = jnp.maximum(jnp.max(sa, axis=-1, keepdims=True), sink_a)
        mb = jnp.maximum(jnp.max(sb, axis=-1, keepdims=True), sink_b)
        pa = jnp.exp(sa - ma)
        pb = jnp.exp(sb - mb)
        da = jnp.sum(pa, axis=-1, keepdims=True) + jnp.exp(sink_a - ma)
        db = jnp.sum(pb, axis=-1, keepdims=True) + jnp.exp(sink_b - mb)
        ps.append(jnp.concatenate([pa, pb], axis=1).astype(BF16))
        lane = lax.broadcasted_iota(jnp.int32, (rows, LANES), 1)
        invs.append(1.0 / jnp.where(lane < HEAD_DIM, da, db))
    o = jnp.dot(jnp.concatenate(ps, axis=0), v_bd, preferred_element_type=F32)
    return o * jnp.concatenate(invs, axis=0)


def _window_mask(rows, q_index, min_key):
    kj = lax.broadcasted_iota(jnp.int32, (rows, KEYS), 1)
    return (kj >= q_index) & (kj <= q_index + WINDOW) & (kj >= min_key)


def _attn_prompt_kernel(sink_ref, q_ref, kvc_ref, kvp_ref, o_ref):
    min_key = jnp.where(pl.program_id(1) == 0, BLOCK, 0)
    kv = jnp.concatenate([kvp_ref[...], kvc_ref[...]], axis=0)
    qi = lax.broadcasted_iota(jnp.int32, (BLOCK, KEYS), 0)
    mask = _window_mask(BLOCK, qi, min_key)
    for h in range(N_KV_HEADS):
        k_bd, v_bd = _kv_block_diag(kv[:, :KV_DIM], kv[:, KV_DIM:], h)
        tiles = range(h * TILES_PER_KV, (h + 1) * TILES_PER_KV)
        q = jnp.concatenate([q_ref[:, t * LANES:(t + 1) * LANES] for t in tiles], axis=0)
        sinks = [(sink_ref[HEADS_PER_TILE * t], sink_ref[HEADS_PER_TILE * t + 1]) for t in tiles]
        o = _attend(q, k_bd, v_bd, mask, sinks).astype(BF16)
        for n, t in enumerate(tiles):
            o_ref[:, t * LANES:(t + 1) * LANES] = o[n * BLOCK:(n + 1) * BLOCK]


def _attn_prompt(q, kv, sinks):
    nb = SEQ // BLOCK
    return pl.pallas_call(
        _attn_prompt_kernel,
        grid=(BATCH, nb),
        in_specs=[
            pl.BlockSpec(memory_space=pltpu.SMEM),
            pl.BlockSpec((BLOCK, D_MODEL), lambda b, n: (b * nb + n, 0)),
            pl.BlockSpec((BLOCK, 2 * KV_DIM), lambda b, n: (b * nb + n, 0)),
            pl.BlockSpec((BLOCK, 2 * KV_DIM), lambda b, n: (b * nb + jnp.maximum(n - 1, 0), 0)),
        ],
        out_specs=pl.BlockSpec((BLOCK, D_MODEL), lambda b, n: (b * nb + n, 0)),
        out_shape=jax.ShapeDtypeStruct((M_PROMPT, D_MODEL), BF16),
        compiler_params=_params("arbitrary", "arbitrary"),
        name="attn_prompt",
    )(sinks, q, kv, kv)


SAMPLE_Q_ROWS = Q_TILES * DEC_SEQ
ROWS_PER_KV = TILES_PER_KV * DEC_SEQ


SAMPLE_BB = 4


def _attn_sample_kernel(sink_ref, q_ref, ck_ref, cv_ref, kn_ref, vn_ref, o_ref, kbuf, vbuf):
    row = lax.broadcasted_iota(jnp.int32, (ROWS_PER_KV, KEYS), 0)
    mask = _window_mask(ROWS_PER_KV, row % DEC_SEQ, 0)
    tile_of_row = lax.broadcasted_iota(jnp.int32, (ROWS_PER_KV, 1), 0) // DEC_SEQ
    sink_cols = []
    for h in range(N_KV_HEADS):
        sink_a = jnp.zeros((ROWS_PER_KV, 1), F32)
        sink_b = jnp.zeros((ROWS_PER_KV, 1), F32)
        for t in range(TILES_PER_KV):
            tile = h * TILES_PER_KV + t
            sink_a = jnp.where(tile_of_row == t, sink_ref[HEADS_PER_TILE * tile], sink_a)
            sink_b = jnp.where(tile_of_row == t, sink_ref[HEADS_PER_TILE * tile + 1], sink_b)
        sink_cols.append((sink_a, sink_b))
    pad = jnp.zeros((KEYS - WINDOW, KV_DIM), F32)
    for b in range(SAMPLE_BB):
        kbuf[b, WINDOW:, :] = pad
        vbuf[b, WINDOW:, :] = pad
        kbuf[b, :WINDOW, :] = ck_ref[b]
        vbuf[b, :WINDOW, :] = cv_ref[b]
        kbuf[b, WINDOW:WINDOW + DEC_SEQ, :] = kn_ref[b]
        vbuf[b, WINDOW:WINDOW + DEC_SEQ, :] = vn_ref[b]
        k_all = kbuf[b]
        v_all = vbuf[b]
        for h in range(N_KV_HEADS):
            k_bd, v_bd = _kv_block_diag(k_all, v_all, h)
            rows = slice(h * ROWS_PER_KV, (h + 1) * ROWS_PER_KV)
            o_ref[b, rows, :] = _attend(q_ref[b, rows, :], k_bd, v_bd, mask, [sink_cols[h]]).astype(BF16)


def _attn_sample(q_s, cache_k, cache_v, k_new, v_new, sinks):
    per_b = lambda b: (b, 0, 0)
    return pl.pallas_call(
        _attn_sample_kernel,
        grid=(DEC_BATCH // SAMPLE_BB,),
        in_specs=[
            pl.BlockSpec(memory_space=pltpu.SMEM),
            pl.BlockSpec((SAMPLE_BB, SAMPLE_Q_ROWS, LANES), per_b),
            pl.BlockSpec((SAMPLE_BB, WINDOW, KV_DIM), per_b),
            pl.BlockSpec((SAMPLE_BB, WINDOW, KV_DIM), per_b),
            pl.BlockSpec((SAMPLE_BB, DEC_SEQ, KV_DIM), per_b),
            pl.BlockSpec((SAMPLE_BB, DEC_SEQ, KV_DIM), per_b),
        ],
        out_specs=pl.BlockSpec((SAMPLE_BB, SAMPLE_Q_ROWS, LANES), per_b),
        out_shape=jax.ShapeDtypeStruct((DEC_BATCH, SAMPLE_Q_ROWS, LANES), BF16),
        scratch_shapes=[pltpu.VMEM((SAMPLE_BB, KEYS, KV_DIM), F32), pltpu.VMEM((SAMPLE_BB, KEYS, KV_DIM), F32)],
        compiler_params=_params("arbitrary"),
        name="attn_sample",
    )(sinks, q_s, cache_k, cache_v, k_new, v_new)


def _prompt_tail_slices():
    out = []
    for b in range(BATCH):
        end = (b + 1) * SEQ
        tile = (end - 1) // TM
        out.append((tile, end - tile * TM - SUBLANES))
    return out


def _conv_mixer_kernel(x_ref, g_ref, wb_ref, wc_ref, wh_ref, cw_ref, p1_ref, p2_ref,
                       a_ref, zp_ref, zs_ref, h_ref, carry_ref):
    i, j = pl.program_id(0), pl.program_id(1)

    @pl.when(j == 0)
    def _():
        h_ref[...] = _rms(x_ref[...], g_ref[...]).astype(BF16)

    @pl.when(i == 0)
    def _():
        carry_ref[j] = jnp.zeros(carry_ref.shape[1:], F32)

    h = h_ref[...]
    c = jnp.dot(h, wc_ref[...], preferred_element_type=F32)
    hin = jnp.dot(h, wh_ref[...], preferred_element_type=F32)
    z = c * hin
    prev = carry_ref[j]
    carry_ref[j] = z[TM - SUBLANES:]

    row = lax.broadcasted_iota(jnp.int32, (TM, 1), 0)
    grow = row + i * TM
    is_sample = grow >= M_PROMPT
    t = jnp.where(is_sample, grow % DEC_SEQ, grow % SEQ)
    top = jnp.zeros((SAMPLE_ROW0, z.shape[1]), F32)
    pre1 = jnp.where(is_sample, jnp.concatenate([top, p1_ref[...]], axis=0), 0.0)
    pre2 = jnp.where(is_sample, jnp.concatenate([top, p2_ref[...]], axis=0), 0.0)
    last = prev[SUBLANES - 1:SUBLANES]
    zm1 = jnp.where(row == 0, last, pltpu.roll(z, 1, axis=0))
    zm2 = jnp.where(row == 0, prev[SUBLANES - 2:SUBLANES - 1],
                    jnp.where(row == 1, last, pltpu.roll(z, 2, axis=0)))
    zm1 = jnp.where(t < 1, pre1, zm1)
    zm2 = jnp.where(t < 2, pre2, zm2)
    y = cw_ref[0:1, :] * zm2 + cw_ref[1:2, :] * zm1 + cw_ref[2:3, :] * z
    b = jnp.dot(h, wb_ref[...], preferred_element_type=F32)
    a_ref[...] = (b * y).astype(BF16)

    zs_ref[...] = z[SAMPLE_ROW0:]
    tails = _prompt_tail_slices()
    zp = z[tails[0][1]:tails[0][1] + SUBLANES]
    for tile, lo in tails[1:]:
        zp = jnp.where(i == tile, z[lo:lo + SUBLANES], zp)
    zp_ref[...] = zp


def _conv_mixer(x, g, w_in, w_conv, p1, p2):
    nj = D_MODEL // CONV_TN
    col = lambda k: (lambda i, j: (0, j + k * nj))
    return pl.pallas_call(
        _conv_mixer_kernel,
        grid=(N_ROW_TILES, nj),
        in_specs=[
            pl.BlockSpec((TM, D_MODEL), lambda i, j: (i, 0)),
            pl.BlockSpec((1, D_MODEL), lambda i, j: (0, 0)),
            pl.BlockSpec((D_MODEL, CONV_TN), col(0)),
            pl.BlockSpec((D_MODEL, CONV_TN), col(1)),
            pl.BlockSpec((D_MODEL, CONV_TN), col(2)),
            pl.BlockSpec((CONV_WIDTH, CONV_TN), lambda i, j: (0, j)),
            pl.BlockSpec((M_SAMPLE, CONV_TN), lambda i, j: (0, j)),
            pl.BlockSpec((M_SAMPLE, CONV_TN), lambda i, j: (0, j)),
        ],
        out_specs=[
            pl.BlockSpec((TM, CONV_TN), lambda i, j: (i, j)),
            pl.BlockSpec((SUBLANES, CONV_TN), lambda i, j: (i, j)),
            pl.BlockSpec((M_SAMPLE, CONV_TN), lambda i, j: (0, jnp.where(i == N_ROW_TILES - 1, j, 0))),
        ],
        out_shape=[
            jax.ShapeDtypeStruct((M_ALL, D_MODEL), BF16),
            jax.ShapeDtypeStruct((N_ROW_TILES * SUBLANES, D_MODEL), F32),
            jax.ShapeDtypeStruct((M_SAMPLE, D_MODEL), F32),
        ],
        scratch_shapes=[pltpu.VMEM((TM, D_MODEL), BF16), pltpu.VMEM((nj, SUBLANES, CONV_TN), F32)],
        compiler_params=_params("arbitrary", "arbitrary"),
        name="conv_mixer",
    )(x, g, w_in, w_in, w_in, w_conv, p1, p2)


def _sgu_mixer_kernel(x_ref, g_ref, w_ref, lg_ref, lb_ref, ws_ref, bst_ref, wss_ref, bsts_ref,
                      a_ref, vln_ref, h_ref, mixed_ref):
    i, j = pl.program_id(0), pl.program_id(1)
    n_chunks = TM // CHUNK

    @pl.when(j == 0)
    def _():
        h = _rms(x_ref[...], g_ref[...]).astype(BF16)
        h_ref[...] = h
        v = jax.nn.gelu(jnp.dot(h, w_ref[...], preferred_element_type=F32))
        mu = jnp.mean(v, axis=-1, keepdims=True)
        xc = v - mu
        vln = xc * lax.rsqrt(jnp.mean(xc * xc, axis=-1, keepdims=True) + LN_EPS) * lg_ref[...] + lb_ref[...]
        vln_ref[...] = vln[SAMPLE_ROW0:]
        vb = vln.astype(BF16)
        r = lax.broadcasted_iota(jnp.int32, (CHUNK, CHUNK), 0)
        c = lax.broadcasted_iota(jnp.int32, (CHUNK, CHUNK), 1)
        causal = r >= c
        last_tile = i == N_ROW_TILES - 1
        for grp in range(SGU_GROUPS):
            cols = slice(grp * SGU_GROUP_DIM, (grp + 1) * SGU_GROUP_DIM)
            w_p = jnp.where(causal, ws_ref[grp], 0.0)
            w_s = jnp.where(last_tile, jnp.where(causal, wss_ref[grp], 0.0), w_p)
            b_p = bst_ref[:, grp:grp + 1]
            b_s = jnp.where(last_tile, bsts_ref[:, grp:grp + 1], b_p)
            for ch in range(n_chunks):
                rows = slice(ch * CHUNK, (ch + 1) * CHUNK)
                sample_chunk = ch == n_chunks - 1
                w = (w_s if sample_chunk else w_p).astype(BF16)
                bias = b_s if sample_chunk else b_p
                mixed_ref[rows, cols] = jnp.dot(w, vb[rows, cols], preferred_element_type=F32) + bias

    @pl.when(j == 1)
    def _():
        u = jax.nn.gelu(jnp.dot(h_ref[...], w_ref[...], preferred_element_type=F32))
        a_ref[...] = (u * mixed_ref[...]).astype(BF16)


def _sgu_mixer(x, g, w_in, ln_g, ln_b, ws, bst, ws_s, bst_s):
    fixed2 = lambda i, j: (0, 0)
    return pl.pallas_call(
        _sgu_mixer_kernel,
        grid=(N_ROW_TILES, 2),
        in_specs=[
            pl.BlockSpec((TM, D_MODEL), lambda i, j: (i, 0)),
            pl.BlockSpec((1, D_MODEL), fixed2),
            pl.BlockSpec((D_MODEL, SGU_WIDTH), lambda i, j: (0, 1 - j)),
            pl.BlockSpec((1, SGU_WIDTH), fixed2),
            pl.BlockSpec((1, SGU_WIDTH), fixed2),
            pl.BlockSpec((SGU_GROUPS, CHUNK, CHUNK), lambda i, j: (0, 0, 0)),
            pl.BlockSpec((CHUNK, SGU_GROUPS), fixed2),
            pl.BlockSpec((SGU_GROUPS, CHUNK, CHUNK), lambda i, j: (0, 0, 0)),
            pl.BlockSpec((CHUNK, SGU_GROUPS), fixed2),
        ],
        out_specs=[
            pl.BlockSpec((TM, SGU_WIDTH), lambda i, j: (i, 0)),
            pl.BlockSpec((M_SAMPLE, SGU_WIDTH), fixed2),
        ],
        out_shape=[
            jax.ShapeDtypeStruct((M_ALL, SGU_WIDTH), BF16),
            jax.ShapeDtypeStruct((M_SAMPLE, SGU_WIDTH), F32),
        ],
        scratch_shapes=[pltpu.VMEM((TM, D_MODEL), BF16), pltpu.VMEM((TM, SGU_WIDTH), F32)],
        compiler_params=_params("arbitrary", "arbitrary"),
        name="sgu_mixer",
    )(x, g, w_in, ln_g, ln_b, ws, bst, ws_s, bst_s)


def _attention_layer(x, g_pre, w_qkv, sinks, cache_k, cache_v, tables):
    q, kv = _attn_proj(x, g_pre, w_qkv, tables)
    o_p = _attn_prompt(q, kv, sinks)

    q_s = q[M_PROMPT:].reshape(DEC_BATCH, DEC_SEQ, Q_TILES, LANES).transpose(0, 2, 1, 3)
    q_s = q_s.reshape(DEC_BATCH, SAMPLE_Q_ROWS, LANES)
    kv_s = kv[M_PROMPT:].reshape(DEC_BATCH, DEC_SEQ, 2 * KV_DIM)
    k_new, v_new = kv_s[..., :KV_DIM], kv_s[..., KV_DIM:]
    ck = cache_k.reshape(DEC_BATCH, WINDOW, KV_DIM)
    cv = cache_v.reshape(DEC_BATCH, WINDOW, KV_DIM)
    o_s = _attn_sample(q_s, ck, cv, k_new, v_new, sinks)
    o_s = o_s.reshape(DEC_BATCH, Q_TILES, DEC_SEQ, LANES).transpose(0, 2, 1, 3).reshape(M_SAMPLE, D_MODEL)

    kv_p = kv[:M_PROMPT].reshape(BATCH, SEQ, 2 * KV_DIM)[:, SEQ - WINDOW:]
    kv_shape = (BATCH, WINDOW, N_KV_HEADS, HEAD_DIM)
    new_kp = kv_p[..., :KV_DIM].reshape(kv_shape)
    new_vp = kv_p[..., KV_DIM:].reshape(kv_shape)
    kv_shape = (DEC_BATCH, WINDOW, N_KV_HEADS, HEAD_DIM)
    new_ks = jnp.concatenate([ck[:, DEC_SEQ:], k_new], axis=1).reshape(kv_shape)
    new_vs = jnp.concatenate([cv[:, DEC_SEQ:], v_new], axis=1).reshape(kv_shape)
    return (o_p, o_s), (new_kp, new_vp, new_ks, new_vs)


def _conv_layer(x, g_pre, w_in, w_conv, state):
    zeros = jnp.zeros((DEC_BATCH, DEC_SEQ - 1, D_MODEL), F32)
    p1 = jnp.concatenate([state[:, 1:2], zeros], axis=1).reshape(M_SAMPLE, D_MODEL)
    p2 = jnp.concatenate([state[:, 0:1], state[:, 1:2], zeros[:, 1:]], axis=1).reshape(M_SAMPLE, D_MODEL)
    a, z_p, z_s = _conv_mixer(x, g_pre, w_in, w_conv, p1, p2)
    keep = CONV_WIDTH - 1
    new_cp = jnp.stack([z_p[(tile + 1) * SUBLANES - keep:(tile + 1) * SUBLANES]
                        for tile, _ in _prompt_tail_slices()])
    new_cs = z_s.reshape(DEC_BATCH, DEC_SEQ, D_MODEL)[:, DEC_SEQ - keep:]
    return a, (new_cp, new_cs)


def _sgu_layer(x, g_pre, w_in, ln_g, ln_b, w_s, b_s):
    eye = jnp.eye(DEC_BATCH, dtype=F32)
    ws_s = jnp.einsum("ab,gts->gatbs", eye, w_s[:, :DEC_SEQ, :DEC_SEQ]).reshape(SGU_GROUPS, M_SAMPLE, M_SAMPLE)
    bst_s = jnp.tile(b_s[:, :DEC_SEQ].T, (DEC_BATCH, 1))
    a, vln_s = _sgu_mixer(x, g_pre, w_in, ln_g, ln_b, w_s, b_s.T, ws_s, bst_s)
    return a, vln_s.reshape(DEC_BATCH, DEC_SEQ, SGU_WIDTH)


def kernel(x_prompt, x_sample, cache_k, cache_v, state_conv, norm_g, ffn_w_gu, ffn_w_down,
           attn_w_qkv, attn_w_o, attn_sinks, conv_w_in, conv_w, conv_w_out,
           sgu_w_in, sgu_ln_g, sgu_ln_b, sgu_w_s, sgu_b_s, sgu_w_out):
    x = (x_prompt.reshape(M_PROMPT, D_MODEL), x_sample.reshape(M_SAMPLE, D_MODEL))
    tables = _rope_tables()
    gain = lambda i, k: norm_g[i, k].reshape(1, D_MODEL)
    mixer_weights = ((attn_w_qkv, attn_w_o), (conv_w_in, conv_w_out), (sgu_w_in, sgu_w_out))
    w_gu, w_down = ffn_w_gu[0, 0].astype(BF16), ffn_w_down[0, 0].astype(BF16)
    kp_l, vp_l, ks_l, vs_l, cp_l, cs_l, sv_l = [], [], [], [], [], [], []
    for i in range(DEPTH):
        kind, j = i % N_MIXERS, i // N_MIXERS
        w_in_f32, w_out_f32 = mixer_weights[kind]
        x, (w_in, w_out, w_gu, w_down) = _ffn(
            x, gain(i, 0), gain(i, 1), w_gu, w_down,
            casts=((w_in_f32, (j,)), (w_out_f32, (j,)), (ffn_w_gu, (i, 1)), (ffn_w_down, (i, 1))))
        if kind == 0:
            a, (kp, vp, ks, vs) = _attention_layer(x, gain(i, 2), w_in, attn_sinks[j],
                                                   cache_k[j], cache_v[j], tables)
            kp_l.append(kp); vp_l.append(vp); ks_l.append(ks); vs_l.append(vs)
        elif kind == 1:
            a, (cp, cs) = _conv_layer(x, gain(i, 2), w_in, conv_w[j], state_conv[j])
            cp_l.append(cp); cs_l.append(cs)
        else:
            a, sv = _sgu_layer(x, gain(i, 2), w_in, sgu_ln_g[j].reshape(1, SGU_WIDTH),
                               sgu_ln_b[j].reshape(1, SGU_WIDTH), sgu_w_s[j], sgu_b_s[j])
            sv_l.append(sv)
        x = _outproj(x, a, gain(i, 3), w_out)
        last = i + 1 == DEPTH
        casts = () if last else ((ffn_w_gu, (i + 1, 0)), (ffn_w_down, (i + 1, 0)))
        x, next_ffn = _ffn(x, gain(i, 4), gain(i, 5), w_gu, w_down, casts=casts, split_out=last)
        if next_ffn:
            w_gu, w_down = next_ffn
    y_prompt = x[0].reshape(BATCH, SEQ, D_MODEL)
    y_sample = x[1].reshape(DEC_BATCH, DEC_SEQ, D_MODEL)
    return (y_prompt, y_sample, jnp.stack(kp_l), jnp.stack(vp_l), jnp.stack(ks_l), jnp.stack(vs_l),
            jnp.stack(cp_l), jnp.stack(cs_l), jnp.stack(sv_l))
```

```python
import functools

import jax
import jax.numpy as jnp
from jax import lax
from jax.experimental import pallas as pl
from jax.experimental.pallas import tpu as pltpu

D_MODEL = 2048
BATCH = 2
SEQ = 4096
DEPTH = 4
DEC_BATCH = 32
DEC_SEQ = 4
PAST_LEN = 16384
HEAD_DIM = 64
N_HEADS = D_MODEL // HEAD_DIM
N_KV_HEADS = N_HEADS // 8
ROT_DIM = HEAD_DIM // 4
ROPE_THETA = 500000.0
WINDOW = 128
BLOCK = 128
CONV_WIDTH = 3
SGU_WIDTH = D_MODEL
SGU_GROUPS = 8
SGU_GROUP_DIM = SGU_WIDTH // SGU_GROUPS
CHUNK = 128
D_FF = 5632
N_MIXERS = 3
RMS_EPS = 1e-6
LN_EPS = 1e-5
NEG_INF = -1e30

M_PROMPT = BATCH * SEQ
M_SAMPLE = DEC_BATCH * DEC_SEQ
M_ALL = M_PROMPT + M_SAMPLE
KV_DIM = N_KV_HEADS * HEAD_DIM
QKV_DIM = D_MODEL + 2 * KV_DIM

LANES = 128
SUBLANES = 8
BF16_SUBLANES = 16
MXU_COLS = 256
HEADS_PER_TILE = LANES // HEAD_DIM
Q_TILES = D_MODEL // LANES
TILES_PER_KV = Q_TILES // N_KV_HEADS
KEYS = 2 * BLOCK

TM = 640
N_ROW_TILES = M_ALL // TM
SAMPLE_ROW0 = M_PROMPT - (N_ROW_TILES - 1) * TM
TF = 512
UP_TM = 1040
UP_TILES = M_ALL // UP_TM
UP_SAMPLE_ROW0 = M_PROMPT - (UP_TILES - 1) * UP_TM
DOWN_TM = 416
DOWN_TILES = M_ALL // DOWN_TM
DOWN_SAMPLE_ROW0 = M_PROMPT - (DOWN_TILES - 1) * DOWN_TM
ROW_CHUNK = 16
CONV_TN = 512
VMEM_LIMIT = 58 * 1024 * 1024

assert M_ALL % TM == 0 and SAMPLE_ROW0 + M_SAMPLE == TM and M_SAMPLE == CHUNK and TM % CHUNK == 0
assert M_PROMPT % DEC_SEQ == 0 and SEQ % CHUNK == 0
assert TM % ROW_CHUNK == 0 and SAMPLE_ROW0 % ROW_CHUNK == 0 and ROW_CHUNK % BF16_SUBLANES == 0
for _tm, _row0 in ((UP_TM, UP_SAMPLE_ROW0), (DOWN_TM, DOWN_SAMPLE_ROW0)):
    assert M_ALL % _tm == 0 and _tm % ROW_CHUNK == 0 and _row0 % ROW_CHUNK == 0 and _row0 + M_SAMPLE == _tm

F32 = jnp.float32
BF16 = jnp.bfloat16


def _rms(x, g):
    return x * lax.rsqrt(jnp.mean(x * x, axis=-1, keepdims=True) + RMS_EPS) * g


def _params(*sem):
    return pltpu.CompilerParams(dimension_semantics=sem, vmem_limit_bytes=VMEM_LIMIT)


def _gain_spec(gain):
    return pl.BlockSpec((None, 1, D_MODEL), lambda *_: (gain[1], 0, 0))


def _wspec(lead, block, idx):
    return pl.BlockSpec((None,) * len(lead) + block, lambda *g: lead + idx(*g))


def _row_chunks(rows):
    return [slice(c * ROW_CHUNK, (c + 1) * ROW_CHUNK) for c in range(rows // ROW_CHUNK)]


def _x_reader(x_ref, xs_ref, n_tiles, sample_row0):
    def read(rows):
        x = x_ref[rows, :]
        if xs_ref is not None and rows.start >= sample_row0:
            tail = xs_ref[rows.start - sample_row0:rows.stop - sample_row0, :]
            x = jnp.where(pl.program_id(0) == n_tiles - 1, tail, x)
        return x
    return read


def _ffn_up_kernel(*refs, n_cast, split_in):
    refs = list(refs)
    x_ref = refs.pop(0)
    xs_ref = refs.pop(0) if split_in else None
    gpre_ref, wg_ref, wu_ref = refs[:3]
    cast_src = refs[3:3 + n_cast]
    a_ref = refs[3 + n_cast]
    cast_dst = refs[4 + n_cast:4 + 2 * n_cast]
    h_ref = refs[-1]
    read_x = _x_reader(x_ref, xs_ref, UP_TILES, UP_SAMPLE_ROW0)

    @pl.when(pl.program_id(1) == 0)
    def _():
        for rows in _row_chunks(UP_TM):
            h_ref[rows, :] = _rms(read_x(rows), gpre_ref[...]).astype(BF16)

    h = h_ref[...]
    for c in range(TF // MXU_COLS):
        cols = slice(c * MXU_COLS, (c + 1) * MXU_COLS)
        g = jnp.dot(h, wg_ref[:, cols], preferred_element_type=F32)
        if c == 0:
            for src, dst in zip(cast_src, cast_dst):
                dst[...] = src[...].astype(BF16)
        u = jnp.dot(h, wu_ref[:, cols], preferred_element_type=F32)
        a_ref[:, cols] = (g * jax.nn.sigmoid(g) * u).astype(BF16)


def _ffn_down_kernel(*refs, split_in, split_out):
    refs = list(refs)
    x_ref = refs.pop(0)
    xs_ref = refs.pop(0) if split_in else None
    a_ref, gpost_ref, wd_ref, o_ref = refs[:4]
    os_ref = refs[4] if split_out else None
    y_ref = refs[-1]
    read_x = _x_reader(x_ref, xs_ref, DOWN_TILES, DOWN_SAMPLE_ROW0)
    y_ref[...] = jnp.dot(a_ref[...], wd_ref[...], preferred_element_type=F32)
    gain = 0.5 * gpost_ref[...]
    for rows in _row_chunks(DOWN_TM):
        y = read_x(rows) + _rms(y_ref[rows, :], gain)
        o_ref[rows, :] = y
        if os_ref is not None and rows.start >= DOWN_SAMPLE_ROW0:
            os_ref[rows.start - DOWN_SAMPLE_ROW0:rows.stop - DOWN_SAMPLE_ROW0, :] = y


def _cast_rows_per_step(rows, steps):
    rps = BF16_SUBLANES
    while rows % rps or rows // rps > steps:
        rps += BF16_SUBLANES
    return rps


def _ffn(x, g_pre, g_post, w_gu, w_down, casts=(), split_out=False):
    split_in = isinstance(x, tuple)
    x_args = list(x) if split_in else [x]
    nj = D_FF // TF

    row_tile = pl.BlockSpec((UP_TM, D_MODEL), lambda i, j: (i, 0))
    sample_rows = pl.BlockSpec((M_SAMPLE, D_MODEL), lambda i, j: (0, 0))
    cast_in_specs, cast_out_specs, cast_shapes, cast_args = [], [], [], []
    for w, lead in casts:
        rows, cols = w.shape[-2:]
        rps = _cast_rows_per_step(rows, UP_TILES * nj)
        slab = lambda i, j, last=rows // rps - 1: (jnp.minimum(i * nj + j, last), 0)
        cast_in_specs.append(_wspec(lead, (rps, cols), slab))
        cast_out_specs.append(pl.BlockSpec((rps, cols), slab))
        cast_shapes.append(jax.ShapeDtypeStruct((rows, cols), BF16))
        cast_args.append(w)
    up = pl.pallas_call(
        functools.partial(_ffn_up_kernel, n_cast=len(casts), split_in=split_in),
        grid=(UP_TILES, nj),
        in_specs=[row_tile] + ([sample_rows] if split_in else []) + [
            _gain_spec(g_pre),
            pl.BlockSpec((D_MODEL, TF), lambda i, j: (0, j)),
            pl.BlockSpec((D_MODEL, TF), lambda i, j: (0, j + nj)),
        ] + cast_in_specs,
        out_specs=[pl.BlockSpec((UP_TM, TF), lambda i, j: (i, j))] + cast_out_specs,
        out_shape=[jax.ShapeDtypeStruct((M_ALL, D_FF), BF16)] + cast_shapes,
        scratch_shapes=[pltpu.VMEM((UP_TM, D_MODEL), BF16)],
        compiler_params=_params("arbitrary", "arbitrary"),
        name="ffn_up",
    )(*x_args, g_pre[0], w_gu, w_gu, *cast_args)
    a, copies = up[0], up[1:]

    row_tile = pl.BlockSpec((DOWN_TM, D_MODEL), lambda i: (i, 0))
    sample_rows = pl.BlockSpec((M_SAMPLE, D_MODEL), lambda i: (0, 0))
    if split_out:
        o_specs = [row_tile, sample_rows]
        o_shapes = [jax.ShapeDtypeStruct((M_PROMPT, D_MODEL), F32), jax.ShapeDtypeStruct((M_SAMPLE, D_MODEL), F32)]
    else:
        o_specs, o_shapes = [row_tile], [jax.ShapeDtypeStruct((M_ALL, D_MODEL), F32)]
    out = pl.pallas_call(
        functools.partial(_ffn_down_kernel, split_in=split_in, split_out=split_out),
        grid=(DOWN_TILES,),
        in_specs=[row_tile] + ([sample_rows] if split_in else []) + [
            pl.BlockSpec((DOWN_TM, D_FF), lambda i: (i, 0)),
            _gain_spec(g_post),
            pl.BlockSpec((D_FF, D_MODEL), lambda i: (0, 0), pipeline_mode=pl.Buffered(1)),
        ],
        out_specs=o_specs,
        out_shape=o_shapes,
        scratch_shapes=[pltpu.VMEM((DOWN_TM, D_MODEL), F32)],
        compiler_params=_params("arbitrary"),
        name="ffn_down",
    )(*x_args, a, g_post[0], w_down)
    return (tuple(out) if split_out else out[0]), copies


def _outproj_kernel(*refs, split_a):
    x_ref, a_ref = refs[:2]
    as_ref = refs[2] if split_a else None
    g_ref, w_ref, o_ref, y_ref = refs[-4:]
    a = a_ref[...]
    if as_ref is not None:
        tail = jnp.where(pl.program_id(0) == N_ROW_TILES - 1, as_ref[...], a_ref[SAMPLE_ROW0:, :])
        a = jnp.concatenate([a_ref[:SAMPLE_ROW0, :], tail], axis=0)
    y_ref[...] = jnp.dot(a, w_ref[...], preferred_element_type=F32)
    for c in range(TM // ROW_CHUNK):
        rows = slice(c * ROW_CHUNK, (c + 1) * ROW_CHUNK)
        o_ref[rows, :] = x_ref[rows, :] + _rms(y_ref[rows, :], g_ref[...])


def _outproj(x, a, g, w):
    row_tile = pl.BlockSpec((TM, D_MODEL), lambda i: (i, 0))
    split_a = isinstance(a, tuple)
    a_args = list(a) if split_a else [a]
    a_specs = [row_tile] + ([pl.BlockSpec((M_SAMPLE, D_MODEL), lambda i: (0, 0))] if split_a else [])
    return pl.pallas_call(
        functools.partial(_outproj_kernel, split_a=split_a),
        grid=(N_ROW_TILES,),
        in_specs=[row_tile] + a_specs + [
            _gain_spec(g),
            pl.BlockSpec((D_MODEL, D_MODEL), lambda i: (0, 0)),
        ],
        out_specs=row_tile,
        out_shape=jax.ShapeDtypeStruct((M_ALL, D_MODEL), F32),
        scratch_shapes=[pltpu.VMEM((TM, D_MODEL), F32)],
        compiler_params=_params("arbitrary"),
        name="outproj",
    )(x, *a_args, g[0], w)


def _rope_tables():
    inv = jnp.power(jnp.float32(ROPE_THETA), -jnp.arange(0, ROT_DIM, 2, dtype=jnp.float32) / ROT_DIM)
    pos = jnp.concatenate([
        jnp.tile(jnp.arange(SEQ), BATCH),
        jnp.tile(PAST_LEN + jnp.arange(DEC_SEQ), DEC_BATCH),
    ])
    ang = pos.astype(jnp.float32)[:, None] * inv[None, :]
    c, s = jnp.cos(ang), jnp.sin(ang)
    pad = jnp.zeros((M_ALL, HEAD_DIM - ROT_DIM), F32)
    zero = jnp.zeros_like(s)
    cos_h = jnp.concatenate([c, c, pad + 1.0], axis=1)
    sa_h = jnp.concatenate([-s, zero, pad], axis=1)
    sb_h = jnp.concatenate([zero, s, pad], axis=1)
    rep = lambda t: jnp.tile(t, (1, HEADS_PER_TILE))
    return rep(cos_h), rep(sa_h), rep(sb_h)


def _attn_proj_kernel(x_ref, g_ref, w_ref, cos_ref, sa_ref, sb_ref, q_ref, kv_ref):
    h = _rms(x_ref[...], g_ref[...]).astype(BF16)
    cos, sa, sb = cos_ref[...], sa_ref[...], sb_ref[...]
    half = ROT_DIM // 2
    for c in range(QKV_DIM // MXU_COLS):
        y2 = jnp.dot(h, w_ref[:, c * MXU_COLS:(c + 1) * MXU_COLS], preferred_element_type=F32)
        for part in range(MXU_COLS // LANES):
            t = c * (MXU_COLS // LANES) + part
            y = y2[:, part * LANES:(part + 1) * LANES]
            if t < Q_TILES + KV_DIM // LANES:
                y = y * cos + pltpu.roll(y, LANES - half, axis=1) * sa + pltpu.roll(y, half, axis=1) * sb
            if t < Q_TILES:
                q_ref[:, t * LANES:(t + 1) * LANES] = (y * (HEAD_DIM ** -0.5)).astype(BF16)
            else:
                kv_ref[:, (t - Q_TILES) * LANES:(t - Q_TILES + 1) * LANES] = y


def _attn_proj(x, g, w, tables):
    row = lambda i: (i, 0)
    fixed = lambda i: (0, 0)
    return pl.pallas_call(
        _attn_proj_kernel,
        grid=(N_ROW_TILES,),
        in_specs=[
            pl.BlockSpec((TM, D_MODEL), row),
            _gain_spec(g),
            pl.BlockSpec((D_MODEL, QKV_DIM), fixed),
            pl.BlockSpec((TM, LANES), row),
            pl.BlockSpec((TM, LANES), row),
            pl.BlockSpec((TM, LANES), row),
        ],
        out_specs=[pl.BlockSpec((TM, D_MODEL), row), pl.BlockSpec((TM, 2 * KV_DIM), row)],
        out_shape=[jax.ShapeDtypeStruct((M_ALL, D_MODEL), BF16),
                   jax.ShapeDtypeStruct((M_ALL, 2 * KV_DIM), F32)],
        compiler_params=_params("arbitrary"),
        name="attn_proj",
    )(x, g[0], w, *tables)


def _both_halves(tile, first_half):
    lane = lax.broadcasted_iota(jnp.int32, tile.shape, 1)
    rolled = pltpu.roll(tile, HEAD_DIM, axis=1)
    keep = (lane < HEAD_DIM) if first_half else (lane >= HEAD_DIM)
    return jnp.where(keep, tile, rolled)


def _block_diag(rep):
    lane = lax.broadcasted_iota(jnp.int32, rep.shape, 1)
    top = jnp.where(lane < HEAD_DIM, rep, 0.0)
    bot = jnp.where(lane >= HEAD_DIM, rep, 0.0)
    return jnp.concatenate([top, bot], axis=0).astype(BF16)


def _kv_block_diag(k_all, v_all, h):
    lo = (h // HEADS_PER_TILE) * LANES
    first_half = h % HEADS_PER_TILE == 0
    return (_block_diag(_both_halves(k_all[:, lo:lo + LANES], first_half)),
            _block_diag(_both_halves(v_all[:, lo:lo + LANES], first_half)))


def _attend(q, k_bd, v_bd, mask, sinks):
    rows = mask.shape[0]
    s_all = lax.dot_general(q, k_bd, (((1,), (1,)), ((), ())), preferred_element_type=F32)
    ps, invs = [], []
    for t, (sink_a, sink_b) in enumerate(sinks):
        s = s_all[t * rows:(t + 1) * rows]
        sa = jnp.where(mask, s[:, :KEYS], NEG_INF)
        sb = jnp.where(mask, s[:, KEYS:], NEG_INF)
        ma = jnp.maximum(jnp.max(sa, axis=-1, keepdims=True), sink_a)
        mb = jnp.maximum(jnp.max(sb, axis=-1, keepdims=True), sink_b)
        pa = jnp.exp(sa - ma)
        pb = jnp.exp(sb - mb)
        da = jnp.sum(pa, axis=-1, keepdims=True) + jnp.exp(sink_a - ma)
        db = jnp.sum(pb, axis=-1, keepdims=True) + jnp.exp(sink_b - mb)
        ps.append(jnp.concatenate([pa, pb], axis=1).astype(BF16))
        lane = lax.broadcasted_iota(jnp.int32, (rows, LANES), 1)
        invs.append(1.0 / jnp.where(lane < HEAD_DIM, da, db))
    o = jnp.dot(jnp.concatenate(ps, axis=0), v_bd, preferred_element_type=F32)
    return o * jnp.concatenate(invs, axis=0)


def _window_mask(rows, q_index, min_key):
    kj = lax.broadcasted_iota(jnp.int32, (rows, KEYS), 1)
    return (kj >= q_index) & (kj <= q_index + WINDOW) & (kj >= min_key)


def _attn_prompt_kernel(sink_ref, q_ref, kvc_ref, kvp_ref, o_ref):
    min_key = jnp.where(pl.program_id(1) == 0, BLOCK, 0)
    kv = jnp.concatenate([kvp_ref[...], kvc_ref[...]], axis=0)
    qi = lax.broadcasted_iota(jnp.int32, (BLOCK, KEYS), 0)
    mask = _window_mask(BLOCK, qi, min_key)
    for h in range(N_KV_HEADS):
        k_bd, v_bd = _kv_block_diag(kv[:, :KV_DIM], kv[:, KV_DIM:], h)
        tiles = range(h * TILES_PER_KV, (h + 1) * TILES_PER_KV)
        q = jnp.concatenate([q_ref[:, t * LANES:(t + 1) * LANES] for t in tiles], axis=0)
        sinks = [(sink_ref[HEADS_PER_TILE * t], sink_ref[HEADS_PER_TILE * t + 1]) for t in tiles]
        o = _attend(q, k_bd, v_bd, mask, sinks).astype(BF16)
        for n, t in enumerate(tiles):
            o_ref[:, t * LANES:(t + 1) * LANES] = o[n * BLOCK:(n + 1) * BLOCK]


def _attn_prompt(q, kv, sinks):
    nb = SEQ // BLOCK
    return pl.pallas_call(
        _attn_prompt_kernel,
        grid=(BATCH, nb),
        in_specs=[
            pl.BlockSpec(memory_space=pltpu.SMEM),
            pl.BlockSpec((BLOCK, D_MODEL), lambda b, n: (b * nb + n, 0)),
            pl.BlockSpec((BLOCK, 2 * KV_DIM), lambda b, n: (b * nb + n, 0)),
            pl.BlockSpec((BLOCK, 2 * KV_DIM), lambda b, n: (b * nb + jnp.maximum(n - 1, 0), 0)),
        ],
        out_specs=pl.BlockSpec((BLOCK, D_MODEL), lambda b, n: (b * nb + n, 0)),
        out_shape=jax.ShapeDtypeStruct((M_PROMPT, D_MODEL), BF16),
        compiler_params=_params("arbitrary", "arbitrary"),
        name="attn_prompt",
    )(sinks, q, kv, kv)


SAMPLE_Q_ROWS = Q_TILES * DEC_SEQ
ROWS_PER_KV = TILES_PER_KV * DEC_SEQ


SAMPLE_BB = 4


def _attn_sample_kernel(sink_ref, q_ref, ck_ref, cv_ref, kn_ref, vn_ref, o_ref, kbuf, vbuf):
    row = lax.broadcasted_iota(jnp.int32, (ROWS_PER_KV, KEYS), 0)
    mask = _window_mask(ROWS_PER_KV, row % DEC_SEQ, 0)
    tile_of_row = lax.broadcasted_iota(jnp.int32, (ROWS_PER_KV, 1), 0) // DEC_SEQ
    sink_cols = []
    for h in range(N_KV_HEADS):
        sink_a = jnp.zeros((ROWS_PER_KV, 1), F32)
        sink_b = jnp.zeros((ROWS_PER_KV, 1), F32)
        for t in range(TILES_PER_KV):
            tile = h * TILES_PER_KV + t
            sink_a = jnp.where(tile_of_row == t, sink_ref[HEADS_PER_TILE * tile], sink_a)
            sink_b = jnp.where(tile_of_row == t, sink_ref[HEADS_PER_TILE * tile + 1], sink_b)
        sink_cols.append((sink_a, sink_b))
    pad = jnp.zeros((KEYS - WINDOW, KV_DIM), F32)
    for b in range(SAMPLE_BB):
        kbuf[b, WINDOW:, :] = pad
        vbuf[b, WINDOW:, :] = pad
        kbuf[b, :WINDOW, :] = ck_ref[b]
        vbuf[b, :WINDOW, :] = cv_ref[b]
        kbuf[b, WINDOW:WINDOW + DEC_SEQ, :] = kn_ref[b]
        vbuf[b, WINDOW:WINDOW + DEC_SEQ, :] = vn_ref[b]
        k_all = kbuf[b]
        v_all = vbuf[b]
        for h in range(N_KV_HEADS):
            k_bd, v_bd = _kv_block_diag(k_all, v_all, h)
            rows = slice(h * ROWS_PER_KV, (h + 1) * ROWS_PER_KV)
            o_ref[b, rows, :] = _attend(q_ref[b, rows, :], k_bd, v_bd, mask, [sink_cols[h]]).astype(BF16)


def _attn_sample(q_s, cache_k, cache_v, k_new, v_new, sinks):
    per_b = lambda b: (b, 0, 0)
    return pl.pallas_call(
        _attn_sample_kernel,
        grid=(DEC_BATCH // SAMPLE_BB,),
        in_specs=[
            pl.BlockSpec(memory_space=pltpu.SMEM),
            pl.BlockSpec((SAMPLE_BB, SAMPLE_Q_ROWS, LANES), per_b),
            pl.BlockSpec((SAMPLE_BB, WINDOW, KV_DIM), per_b),
            pl.BlockSpec((SAMPLE_BB, WINDOW, KV_DIM), per_b),
            pl.BlockSpec((SAMPLE_BB, DEC_SEQ, KV_DIM), per_b),
            pl.BlockSpec((SAMPLE_BB, DEC_SEQ, KV_DIM), per_b),
        ],
        out_specs=pl.BlockSpec((SAMPLE_BB, SAMPLE_Q_ROWS, LANES), per_b),
        out_shape=jax.ShapeDtypeStruct((DEC_BATCH, SAMPLE_Q_ROWS, LANES), BF16),
        scratch_shapes=[pltpu.VMEM((SAMPLE_BB, KEYS, KV_DIM), F32), pltpu.VMEM((SAMPLE_BB, KEYS, KV_DIM), F32)],
        compiler_params=_params("arbitrary"),
        name="attn_sample",
    )(sinks, q_s, cache_k, cache_v, k_new, v_new)


def _prompt_tail_slices():
    out = []
    for b in range(BATCH):
        end = (b + 1) * SEQ
        tile = (end - 1) // TM
        out.append((tile, end - tile * TM - SUBLANES))
    return out


def _conv_mixer_kernel(x_ref, g_ref, wb_ref, wc_ref, wh_ref, cw_ref, p1_ref, p2_ref,
                       a_ref, zp_ref, zs_ref, h_ref, carry_ref):
    i, j = pl.program_id(0), pl.program_id(1)

    @pl.when(j == 0)
    def _():
        h_ref[...] = _rms(x_ref[...], g_ref[...]).astype(BF16)

    @pl.when(i == 0)
    def _():
        carry_ref[j] = jnp.zeros(carry_ref.shape[1:], F32)

    h = h_ref[...]
    c = jnp.dot(h, wc_ref[...], preferred_element_type=F32)
    hin = jnp.dot(h, wh_ref[...], preferred_element_type=F32)
    z = c * hin
    prev = carry_ref[j]
    carry_ref[j] = z[TM - SUBLANES:]

    row = lax.broadcasted_iota(jnp.int32, (TM, 1), 0)
    grow = row + i * TM
    is_sample = grow >= M_PROMPT
    t = jnp.where(is_sample, grow % DEC_SEQ, grow % SEQ)
    top = jnp.zeros((SAMPLE_ROW0, z.shape[1]), F32)
    pre1 = jnp.where(is_sample, jnp.concatenate([top, p1_ref[...]], axis=0), 0.0)
    pre2 = jnp.where(is_sample, jnp.concatenate([top, p2_ref[...]], axis=0), 0.0)
    last = prev[SUBLANES - 1:SUBLANES]
    zm1 = jnp.where(row == 0, last, pltpu.roll(z, 1, axis=0))
    zm2 = jnp.where(row == 0, prev[SUBLANES - 2:SUBLANES - 1],
                    jnp.where(row == 1, last, pltpu.roll(z, 2, axis=0)))
    zm1 = jnp.where(t < 1, pre1, zm1)
    zm2 = jnp.where(t < 2, pre2, zm2)
    y = cw_ref[0:1, :] * zm2 + cw_ref[1:2, :] * zm1 + cw_ref[2:3, :] * z
    b = jnp.dot(h, wb_ref[...], preferred_element_type=F32)
    a_ref[...] = (b * y).astype(BF16)

    zs_ref[...] = z[SAMPLE_ROW0:]
    tails = _prompt_tail_slices()
    zp = z[tails[0][1]:tails[0][1] + SUBLANES]
    for tile, lo in tails[1:]:
        zp = jnp.where(i == tile, z[lo:lo + SUBLANES], zp)
    zp_ref[...] = zp


def _conv_mixer(x, g, w_in, w_conv, p1, p2):
    nj = D_MODEL // CONV_TN
    col = lambda k: (lambda i, j: (0, j + k * nj))
    return pl.pallas_call(
        _conv_mixer_kernel,
        grid=(N_ROW_TILES, nj),
        in_specs=[
            pl.BlockSpec((TM, D_MODEL), lambda i, j: (i, 0)),
            _gain_spec(g),
            pl.BlockSpec((D_MODEL, CONV_TN), col(0)),
            pl.BlockSpec((D_MODEL, CONV_TN), col(1)),
            pl.BlockSpec((D_MODEL, CONV_TN), col(2)),
            pl.BlockSpec((CONV_WIDTH, CONV_TN), lambda i, j: (0, j)),
            pl.BlockSpec((M_SAMPLE, CONV_TN), lambda i, j: (0, j)),
            pl.BlockSpec((M_SAMPLE, CONV_TN), lambda i, j: (0, j)),
        ],
        out_specs=[
            pl.BlockSpec((TM, CONV_TN), lambda i, j: (i, j)),
            pl.BlockSpec((SUBLANES, CONV_TN), lambda i, j: (i, j)),
            pl.BlockSpec((M_SAMPLE, CONV_TN), lambda i, j: (0, jnp.where(i == N_ROW_TILES - 1, j, 0))),
        ],
        out_shape=[
            jax.ShapeDtypeStruct((M_ALL, D_MODEL), BF16),
            jax.ShapeDtypeStruct((N_ROW_TILES * SUBLANES, D_MODEL), F32),
            jax.ShapeDtypeStruct((M_SAMPLE, D_MODEL), F32),
        ],
        scratch_shapes=[pltpu.VMEM((TM, D_MODEL), BF16), pltpu.VMEM((nj, SUBLANES, CONV_TN), F32)],
        compiler_params=_params("arbitrary", "arbitrary"),
        name="conv_mixer",
    )(x, g[0], w_in, w_in, w_in, w_conv, p1, p2)


def _sgu_mixer_kernel(x_ref, g_ref, w_ref, lg_ref, lb_ref, ws_ref, bst_ref, wss_ref, bsts_ref,
                      a_ref, vln_ref, h_ref, mixed_ref):
    i, j = pl.program_id(0), pl.program_id(1)
    n_chunks = TM // CHUNK

    @pl.when(j == 0)
    def _():
        h = _rms(x_ref[...], g_ref[...]).astype(BF16)
        h_ref[...] = h
        v = jax.nn.gelu(jnp.dot(h, w_ref[...], preferred_element_type=F32))
        mu = jnp.mean(v, axis=-1, keepdims=True)
        xc = v - mu
        vln = xc * lax.rsqrt(jnp.mean(xc * xc, axis=-1, keepdims=True) + LN_EPS) * lg_ref[...] + lb_ref[...]
        vln_ref[...] = vln[SAMPLE_ROW0:]
        vb = vln.astype(BF16)
        r = lax.broadcasted_iota(jnp.int32, (CHUNK, CHUNK), 0)
        c = lax.broadcasted_iota(jnp.int32, (CHUNK, CHUNK), 1)
        causal = r >= c
        last_tile = i == N_ROW_TILES - 1
        for grp in range(SGU_GROUPS):
            cols = slice(grp * SGU_GROUP_DIM, (grp + 1) * SGU_GROUP_DIM)
            w_p = jnp.where(causal, ws_ref[grp], 0.0)
            w_s = jnp.where(last_tile, jnp.where(causal, wss_ref[grp], 0.0), w_p)
            b_p = bst_ref[:, grp:grp + 1]
            b_s = jnp.where(last_tile, bsts_ref[:, grp:grp + 1], b_p)
            for ch in range(n_chunks):
                rows = slice(ch * CHUNK, (ch + 1) * CHUNK)
                sample_chunk = ch == n_chunks - 1
                w = (w_s if sample_chunk else w_p).astype(BF16)
                bias = b_s if sample_chunk else b_p
                mixed_ref[rows, cols] = jnp.dot(w, vb[rows, cols], preferred_element_type=F32) + bias

    @pl.when(j == 1)
    def _():
        u = jax.nn.gelu(jnp.dot(h_ref[...], w_ref[...], preferred_element_type=F32))
        a_ref[...] = (u * mixed_ref[...]).astype(BF16)


def _sgu_mixer(x, g, w_in, ln_g, ln_b, ws, bst, ws_s, bst_s):
    fixed2 = lambda i, j: (0, 0)
    return pl.pallas_call(
        _sgu_mixer_kernel,
        grid=(N_ROW_TILES, 2),
        in_specs=[
            pl.BlockSpec((TM, D_MODEL), lambda i, j: (i, 0)),
            _gain_spec(g),
            pl.BlockSpec((D_MODEL, SGU_WIDTH), lambda i, j: (0, 1 - j)),
            pl.BlockSpec((1, SGU_WIDTH), fixed2),
            pl.BlockSpec((1, SGU_WIDTH), fixed2),
            pl.BlockSpec((SGU_GROUPS, CHUNK, CHUNK), lambda i, j: (0, 0, 0)),
            pl.BlockSpec((CHUNK, SGU_GROUPS), fixed2),
            pl.BlockSpec((SGU_GROUPS, CHUNK, CHUNK), lambda i, j: (0, 0, 0)),
            pl.BlockSpec((CHUNK, SGU_GROUPS), fixed2),
        ],
        out_specs=[
            pl.BlockSpec((TM, SGU_WIDTH), lambda i, j: (i, 0)),
            pl.BlockSpec((M_SAMPLE, SGU_WIDTH), fixed2),
        ],
        out_shape=[
            jax.ShapeDtypeStruct((M_ALL, SGU_WIDTH), BF16),
            jax.ShapeDtypeStruct((M_SAMPLE, SGU_WIDTH), F32),
        ],
        scratch_shapes=[pltpu.VMEM((TM, D_MODEL), BF16), pltpu.VMEM((TM, SGU_WIDTH), F32)],
        compiler_params=_params("arbitrary", "arbitrary"),
        name="sgu_mixer",
    )(x, g[0], w_in, ln_g, ln_b, ws, bst, ws_s, bst_s)


def _attention_layer(x, g_pre, w_qkv, sinks, cache_k, cache_v, tables):
    q, kv = _attn_proj(x, g_pre, w_qkv, tables)
    o_p = _attn_prompt(q, kv, sinks)

    q_s = q[M_PROMPT:].reshape(DEC_BATCH, DEC_SEQ, Q_TILES, LANES).transpose(0, 2, 1, 3)
    q_s = q_s.reshape(DEC_BATCH, SAMPLE_Q_ROWS, LANES)
    kv_s = kv[M_PROMPT:].reshape(DEC_BATCH, DEC_SEQ, 2 * KV_DIM)
    k_new, v_new = kv_s[..., :KV_DIM], kv_s[..., KV_DIM:]
    ck = cache_k.reshape(DEC_BATCH, WINDOW, KV_DIM)
    cv = cache_v.reshape(DEC_BATCH, WINDOW, KV_DIM)
    o_s = _attn_sample(q_s, ck, cv, k_new, v_new, sinks)
    o_s = o_s.reshape(DEC_BATCH, Q_TILES, DEC_SEQ, LANES).transpose(0, 2, 1, 3).reshape(M_SAMPLE, D_MODEL)

    kv_p = kv[:M_PROMPT].reshape(BATCH, SEQ, 2 * KV_DIM)[:, SEQ - WINDOW:]
    kv_shape = (BATCH, WINDOW, N_KV_HEADS, HEAD_DIM)
    new_kp = kv_p[..., :KV_DIM].reshape(kv_shape)
    new_vp = kv_p[..., KV_DIM:].reshape(kv_shape)
    kv_shape = (DEC_BATCH, WINDOW, N_KV_HEADS, HEAD_DIM)
    new_ks = jnp.concatenate([ck[:, DEC_SEQ:], k_new], axis=1).reshape(kv_shape)
    new_vs = jnp.concatenate([cv[:, DEC_SEQ:], v_new], axis=1).reshape(kv_shape)
    return (o_p, o_s), (new_kp, new_vp, new_ks, new_vs)


def _conv_layer(x, g_pre, w_in, w_conv, state):
    zeros = jnp.zeros((DEC_BATCH, DEC_SEQ - 1, D_MODEL), F32)
    p1 = jnp.concatenate([state[:, 1:2], zeros], axis=1).reshape(M_SAMPLE, D_MODEL)
    p2 = jnp.concatenate([state[:, 0:1], state[:, 1:2], zeros[:, 1:]], axis=1).reshape(M_SAMPLE, D_MODEL)
    a, z_p, z_s = _conv_mixer(x, g_pre, w_in, w_conv, p1, p2)
    keep = CONV_WIDTH - 1
    new_cp = jnp.stack([z_p[(tile + 1) * SUBLANES - keep:(tile + 1) * SUBLANES]
                        for tile, _ in _prompt_tail_slices()])
    new_cs = z_s.reshape(DEC_BATCH, DEC_SEQ, D_MODEL)[:, DEC_SEQ - keep:]
    return a, (new_cp, new_cs)


def _sgu_layer(x, g_pre, w_in, ln_g, ln_b, w_s, b_s):
    eye = jnp.eye(DEC_BATCH, dtype=F32)
    ws_s = jnp.einsum("ab,gts->gatbs", eye, w_s[:, :DEC_SEQ, :DEC_SEQ]).reshape(SGU_GROUPS, M_SAMPLE, M_SAMPLE)
    bst_s = jnp.tile(b_s[:, :DEC_SEQ].T, (DEC_BATCH, 1))
    a, vln_s = _sgu_mixer(x, g_pre, w_in, ln_g, ln_b, w_s, b_s.T, ws_s, bst_s)
    return a, vln_s.reshape(DEC_BATCH, DEC_SEQ, SGU_WIDTH)


def kernel(x_prompt, x_sample, cache_k, cache_v, state_conv, norm_g, ffn_w_gu, ffn_w_down,
           attn_w_qkv, attn_w_o, attn_sinks, conv_w_in, conv_w, conv_w_out,
           sgu_w_in, sgu_ln_g, sgu_ln_b, sgu_w_s, sgu_b_s, sgu_w_out):
    x = (x_prompt.reshape(M_PROMPT, D_MODEL), x_sample.reshape(M_SAMPLE, D_MODEL))
    tables = _rope_tables()
    gains = norm_g.reshape(-1, 1, D_MODEL)
    gain = lambda i, k: (gains, i * norm_g.shape[1] + k)
    mixer_weights = ((attn_w_qkv, attn_w_o), (conv_w_in, conv_w_out), (sgu_w_in, sgu_w_out))
    w_gu, w_down = ffn_w_gu[0, 0].astype(BF16), ffn_w_down[0, 0].astype(BF16)
    kp_l, vp_l, ks_l, vs_l, cp_l, cs_l, sv_l = [], [], [], [], [], [], []
    for i in range(DEPTH):
        kind, j = i % N_MIXERS, i // N_MIXERS
        w_in_f32, w_out_f32 = mixer_weights[kind]
        x, (w_in, w_out, w_gu, w_down) = _ffn(
            x, gain(i, 0), gain(i, 1), w_gu, w_down,
            casts=((w_in_f32, (j,)), (w_out_f32, (j,)), (ffn_w_gu, (i, 1)), (ffn_w_down, (i, 1))))
        if kind == 0:
            a, (kp, vp, ks, vs) = _attention_layer(x, gain(i, 2), w_in, attn_sinks[j],
                                                   cache_k[j], cache_v[j], tables)
            kp_l.append(kp); vp_l.append(vp); ks_l.append(ks); vs_l.append(vs)
        elif kind == 1:
            a, (cp, cs) = _conv_layer(x, gain(i, 2), w_in, conv_w[j], state_conv[j])
            cp_l.append(cp); cs_l.append(cs)
        else:
            a, sv = _sgu_layer(x, gain(i, 2), w_in, sgu_ln_g[j].reshape(1, SGU_WIDTH),
                               sgu_ln_b[j].reshape(1, SGU_WIDTH), sgu_w_s[j], sgu_b_s[j])
            sv_l.append(sv)
        x = _outproj(x, a, gain(i, 3), w_out)
        last = i + 1 == DEPTH
        casts = () if last else ((ffn_w_gu, (i + 1, 0)), (ffn_w_down, (i + 1, 0)))
        x, next_ffn = _ffn(x, gain(i, 4), gain(i, 5), w_gu, w_down, casts=casts, split_out=last)
        if next_ffn:
            w_gu, w_down = next_ffn
    y_prompt = x[0].reshape(BATCH, SEQ, D_MODEL)
    y_sample = x[1].reshape(DEC_BATCH, DEC_SEQ, D_MODEL)
    return (y_prompt, y_sample, jnp.stack(kp_l), jnp.stack(vp_l), jnp.stack(ks_l), jnp.stack(vs_l),
            jnp.stack(cp_l), jnp.stack(cs_l), jnp.stack(sv_l))
```

```python
import functools

import jax
import jax.numpy as jnp
from jax import lax
from jax.experimental import pallas as pl
from jax.experimental.pallas import tpu as pltpu

D_MODEL = 2048
BATCH = 2
SEQ = 4096
DEPTH = 4
DEC_BATCH = 32
DEC_SEQ = 4
PAST_LEN = 16384
HEAD_DIM = 64
N_HEADS = D_MODEL // HEAD_DIM
N_KV_HEADS = N_HEADS // 8
ROT_DIM = HEAD_DIM // 4
ROPE_THETA = 500000.0
WINDOW = 128
BLOCK = 128
CONV_WIDTH = 3
SGU_WIDTH = D_MODEL
SGU_GROUPS = 8
SGU_GROUP_DIM = SGU_WIDTH // SGU_GROUPS
CHUNK = 128
D_FF = 5632
N_MIXERS = 3
RMS_EPS = 1e-6
LN_EPS = 1e-5
NEG_INF = -1e30
LOG2E = 1.4426950408889634

M_PROMPT = BATCH * SEQ
M_SAMPLE = DEC_BATCH * DEC_SEQ
M_ALL = M_PROMPT + M_SAMPLE
KV_DIM = N_KV_HEADS * HEAD_DIM
QKV_DIM = D_MODEL + 2 * KV_DIM

LANES = 128
SUBLANES = 8
BF16_SUBLANES = 16
MXU_COLS = 256
HEADS_PER_TILE = LANES // HEAD_DIM
Q_TILES = D_MODEL // LANES
TILES_PER_KV = Q_TILES // N_KV_HEADS
KEYS = 2 * BLOCK

TM = 640
N_ROW_TILES = M_ALL // TM
SAMPLE_ROW0 = M_PROMPT - (N_ROW_TILES - 1) * TM
TF = 512
ROW_CHUNK = 16
CONV_TN = 512
VMEM_LIMIT = 58 * 1024 * 1024

assert M_ALL % TM == 0 and SAMPLE_ROW0 + M_SAMPLE == TM and M_SAMPLE == CHUNK and TM % CHUNK == 0
assert M_PROMPT % DEC_SEQ == 0 and SEQ % CHUNK == 0
assert TM % ROW_CHUNK == 0 and SAMPLE_ROW0 % ROW_CHUNK == 0 and ROW_CHUNK % BF16_SUBLANES == 0

F32 = jnp.float32
BF16 = jnp.bfloat16


def _rms(x, g):
    return x * lax.rsqrt(jnp.mean(x * x, axis=-1, keepdims=True) + RMS_EPS) * g


def _params(*sem):
    return pltpu.CompilerParams(dimension_semantics=sem, vmem_limit_bytes=VMEM_LIMIT)


def _gain_spec(gain):
    return pl.BlockSpec((None, 1, D_MODEL), lambda *_: (gain[1], 0, 0))


def _wspec(lead, block, idx):
    return pl.BlockSpec((None,) * len(lead) + block, lambda *g: lead + idx(*g))


def _row_chunks(rows):
    return [slice(c * ROW_CHUNK, (c + 1) * ROW_CHUNK) for c in range(rows // ROW_CHUNK)]


def _x_reader(x_ref, xs_ref, n_tiles, sample_row0):
    def read(rows):
        x = x_ref[rows, :]
        if xs_ref is not None and rows.start >= sample_row0:
            tail = xs_ref[rows.start - sample_row0:rows.stop - sample_row0, :]
            x = jnp.where(pl.program_id(0) == n_tiles - 1, tail, x)
        return x
    return read


def _ffn_kernel(*refs, nj, n_cast, split_in, split_out):
    refs = list(refs)
    x_ref = refs.pop(0)
    xs_ref = refs.pop(0) if split_in else None
    gpre_ref, gpost_ref, wg_ref, wu_ref, wd_ref = refs[:5]
    cast_src = refs[5:5 + n_cast]
    refs = refs[5 + n_cast:]
    o_ref = refs.pop(0)
    os_ref = refs.pop(0) if split_out else None
    cast_dst = refs[:n_cast]
    h_ref, acc_ref = refs[n_cast:]
    i, j = pl.program_id(0), pl.program_id(1)
    read_x = _x_reader(x_ref, xs_ref, N_ROW_TILES, SAMPLE_ROW0)

    @pl.when((i == 0) & (j == 0))
    def _():
        acc_ref[...] = jnp.zeros_like(acc_ref)

    @pl.when(j == 0)
    def _():
        for rows in _row_chunks(TM):
            h_ref[rows, :] = _rms(read_x(rows), gpre_ref[...]).astype(BF16)

    h = h_ref[...]
    g = jnp.dot(h, wg_ref[...], preferred_element_type=F32)
    for src, dst in zip(cast_src, cast_dst):
        dst[...] = src[...].astype(BF16)
    u = jnp.dot(h, wu_ref[...], preferred_element_type=F32)
    a = (g * jax.nn.sigmoid(g) * u).astype(BF16)
    d = jnp.dot(a, wd_ref[...], preferred_element_type=F32)
    acc_ref[...] = jnp.where(j == 0, 0.0, acc_ref[...]) + d

    @pl.when(j == nj - 1)
    def _():
        gain = 0.5 * gpost_ref[...]
        for rows in _row_chunks(TM):
            y = read_x(rows) + _rms(acc_ref[rows, :], gain)
            o_ref[rows, :] = y
            if os_ref is not None and rows.start >= SAMPLE_ROW0:
                os_ref[rows.start - SAMPLE_ROW0:rows.stop - SAMPLE_ROW0, :] = y


def _cast_rows_per_step(rows, steps):
    rps = BF16_SUBLANES
    while rows % rps or rows // rps > steps:
        rps += BF16_SUBLANES
    return rps


def _ffn(x, g_pre, g_post, w_gu, w_down, casts=(), split_out=False):
    ni, nj = N_ROW_TILES, D_FF // TF
    row_tile = pl.BlockSpec((TM, D_MODEL), lambda i, j: (i, 0))
    sample_rows = pl.BlockSpec((M_SAMPLE, D_MODEL), lambda i, j: (0, 0))
    split_in = isinstance(x, tuple)
    x_args, x_specs = (list(x), [row_tile, sample_rows]) if split_in else ([x], [row_tile])
    if split_out:
        o_specs = [row_tile, sample_rows]
        o_shapes = [jax.ShapeDtypeStruct((M_PROMPT, D_MODEL), F32), jax.ShapeDtypeStruct((M_SAMPLE, D_MODEL), F32)]
    else:
        o_specs, o_shapes = [row_tile], [jax.ShapeDtypeStruct((M_ALL, D_MODEL), F32)]
    cast_in_specs, cast_out_specs, cast_shapes, cast_args = [], [], [], []
    for w, lead in casts:
        rows, cols = w.shape[-2:]
        rps = _cast_rows_per_step(rows, ni * nj)
        slab = lambda i, j, last=rows // rps - 1: (jnp.minimum(i * nj + j, last), 0)
        cast_in_specs.append(_wspec(lead, (rps, cols), slab))
        cast_out_specs.append(pl.BlockSpec((rps, cols), slab))
        cast_shapes.append(jax.ShapeDtypeStruct((rows, cols), BF16))
        cast_args.append(w)
    out = pl.pallas_call(
        functools.partial(_ffn_kernel, nj=nj, n_cast=len(casts), split_in=split_in, split_out=split_out),
        grid=(ni, nj),
        in_specs=x_specs + [
            _gain_spec(g_pre),
            _gain_spec(g_post),
            pl.BlockSpec((D_MODEL, TF), lambda i, j: (0, j)),
            pl.BlockSpec((D_MODEL, TF), lambda i, j: (0, j + nj)),
            pl.BlockSpec((TF, D_MODEL), lambda i, j: (j, 0)),
        ] + cast_in_specs,
        out_specs=o_specs + cast_out_specs,
        out_shape=o_shapes + cast_shapes,
        scratch_shapes=[pltpu.VMEM((TM, D_MODEL), BF16), pltpu.VMEM((TM, D_MODEL), F32)],
        compiler_params=_params("arbitrary", "arbitrary"),
        name="ffn",
    )(*x_args, g_pre[0], g_post[0], w_gu, w_gu, w_down, *cast_args)
    n_x = len(o_specs)
    return (tuple(out[:n_x]) if split_out else out[0]), out[n_x:]


def _outproj_kernel(*refs, split_a):
    x_ref, a_ref = refs[:2]
    as_ref = refs[2] if split_a else None
    g_ref, w_ref, o_ref, y_ref = refs[-4:]
    a = a_ref[...]
    if as_ref is not None:
        tail = jnp.where(pl.program_id(0) == N_ROW_TILES - 1, as_ref[...], a_ref[SAMPLE_ROW0:, :])
        a = jnp.concatenate([a_ref[:SAMPLE_ROW0, :], tail], axis=0)
    y_ref[...] = jnp.dot(a, w_ref[...], preferred_element_type=F32)
    for c in range(TM // ROW_CHUNK):
        rows = slice(c * ROW_CHUNK, (c + 1) * ROW_CHUNK)
        o_ref[rows, :] = x_ref[rows, :] + _rms(y_ref[rows, :], g_ref[...])


def _outproj(x, a, g, w):
    row_tile = pl.BlockSpec((TM, D_MODEL), lambda i: (i, 0))
    split_a = isinstance(a, tuple)
    a_args = list(a) if split_a else [a]
    a_specs = [row_tile] + ([pl.BlockSpec((M_SAMPLE, D_MODEL), lambda i: (0, 0))] if split_a else [])
    return pl.pallas_call(
        functools.partial(_outproj_kernel, split_a=split_a),
        grid=(N_ROW_TILES,),
        in_specs=[row_tile] + a_specs + [
            _gain_spec(g),
            pl.BlockSpec((D_MODEL, D_MODEL), lambda i: (0, 0)),
        ],
        out_specs=row_tile,
        out_shape=jax.ShapeDtypeStruct((M_ALL, D_MODEL), F32),
        scratch_shapes=[pltpu.VMEM((TM, D_MODEL), F32)],
        compiler_params=_params("arbitrary"),
        name="outproj",
    )(x, *a_args, g[0], w)


def _rope_tables():
    inv = jnp.power(jnp.float32(ROPE_THETA), -jnp.arange(0, ROT_DIM, 2, dtype=jnp.float32) / ROT_DIM)
    pos = jnp.concatenate([
        jnp.tile(jnp.arange(SEQ), BATCH),
        jnp.tile(PAST_LEN + jnp.arange(DEC_SEQ), DEC_BATCH),
    ])
    ang = pos.astype(jnp.float32)[:, None] * inv[None, :]
    c, s = jnp.cos(ang), jnp.sin(ang)
    pad = jnp.zeros((M_ALL, HEAD_DIM - ROT_DIM), F32)
    zero = jnp.zeros_like(s)
    cos_h = jnp.concatenate([c, c, pad + 1.0], axis=1)
    sa_h = jnp.concatenate([-s, zero, pad], axis=1)
    sb_h = jnp.concatenate([zero, s, pad], axis=1)
    rep = lambda t: jnp.tile(t, (1, HEADS_PER_TILE))
    return rep(cos_h), rep(sa_h), rep(sb_h)


def _attn_proj_kernel(x_ref, g_ref, w_ref, cos_ref, sa_ref, sb_ref, q_ref, kv_ref):
    h = _rms(x_ref[...], g_ref[...]).astype(BF16)
    cos, sa, sb = cos_ref[...], sa_ref[...], sb_ref[...]
    half = ROT_DIM // 2
    for c in range(QKV_DIM // MXU_COLS):
        y2 = jnp.dot(h, w_ref[:, c * MXU_COLS:(c + 1) * MXU_COLS], preferred_element_type=F32)
        for part in range(MXU_COLS // LANES):
            t = c * (MXU_COLS // LANES) + part
            y = y2[:, part * LANES:(part + 1) * LANES]
            if t < Q_TILES + KV_DIM // LANES:
                y = y * cos + pltpu.roll(y, LANES - half, axis=1) * sa + pltpu.roll(y, half, axis=1) * sb
            if t < Q_TILES:
                q_ref[:, t * LANES:(t + 1) * LANES] = (y * (HEAD_DIM ** -0.5 * LOG2E)).astype(BF16)
            else:
                kv_ref[:, (t - Q_TILES) * LANES:(t - Q_TILES + 1) * LANES] = y


def _attn_proj(x, g, w, tables):
    row = lambda i: (i, 0)
    fixed = lambda i: (0, 0)
    return pl.pallas_call(
        _attn_proj_kernel,
        grid=(N_ROW_TILES,),
        in_specs=[
            pl.BlockSpec((TM, D_MODEL), row),
            _gain_spec(g),
            pl.BlockSpec((D_MODEL, QKV_DIM), fixed),
            pl.BlockSpec((TM, LANES), row),
            pl.BlockSpec((TM, LANES), row),
            pl.BlockSpec((TM, LANES), row),
        ],
        out_specs=[pl.BlockSpec((TM, D_MODEL), row), pl.BlockSpec((TM, 2 * KV_DIM), row)],
        out_shape=[jax.ShapeDtypeStruct((M_ALL, D_MODEL), BF16),
                   jax.ShapeDtypeStruct((M_ALL, 2 * KV_DIM), F32)],
        compiler_params=_params("arbitrary"),
        name="attn_proj",
    )(x, g[0], w, *tables)


def _both_halves(tile, first_half):
    lane = lax.broadcasted_iota(jnp.int32, tile.shape, 1)
    rolled = pltpu.roll(tile, HEAD_DIM, axis=1)
    keep = (lane < HEAD_DIM) if first_half else (lane >= HEAD_DIM)
    return jnp.where(keep, tile, rolled)


def _block_diag(rep):
    lane = lax.broadcasted_iota(jnp.int32, rep.shape, 1)
    top = jnp.where(lane < HEAD_DIM, rep, 0.0)
    bot = jnp.where(lane >= HEAD_DIM, rep, 0.0)
    return jnp.concatenate([top, bot], axis=0).astype(BF16)


def _kv_block_diag(k_all, v_all, h):
    lo = (h // HEADS_PER_TILE) * LANES
    first_half = h % HEADS_PER_TILE == 0
    return (_block_diag(_both_halves(k_all[:, lo:lo + LANES], first_half)),
            _block_diag(_both_halves(v_all[:, lo:lo + LANES], first_half)))


def _attend(q, k_bd, v_bd, mask, sinks):
    rows = mask.shape[0]
    s_all = lax.dot_general(q, k_bd, (((1,), (1,)), ((), ())), preferred_element_type=F32)
    lane = lax.broadcasted_iota(jnp.int32, (rows, LANES), 1)
    ps, sink_terms = [], []
    for t, (sink_a, sink_b) in enumerate(sinks):
        s = s_all[t * rows:(t + 1) * rows]
        sa = jnp.where(mask, s[:, :KEYS], NEG_INF)
        sb = jnp.where(mask, s[:, KEYS:], NEG_INF)
        ma = jnp.maximum(jnp.max(sa, axis=-1, keepdims=True), sink_a * LOG2E)
        mb = jnp.maximum(jnp.max(sb, axis=-1, keepdims=True), sink_b * LOG2E)
        ps.append(jnp.concatenate([jnp.exp2(sa - ma), jnp.exp2(sb - mb)], axis=1).astype(BF16))
        sink_terms.append(jnp.where(lane < HEAD_DIM, jnp.exp2(sink_a * LOG2E - ma), jnp.exp2(sink_b * LOG2E - mb)))
    key = lax.broadcasted_iota(jnp.int32, v_bd.shape, 0)
    lane_v = lax.broadcasted_iota(jnp.int32, v_bd.shape, 1)
    ones_bd = jnp.where((key < KEYS) == (lane_v < HEAD_DIM), 1.0, 0.0).astype(BF16)
    o = jnp.dot(jnp.concatenate(ps, axis=0), jnp.concatenate([v_bd, ones_bd], axis=1),
                preferred_element_type=F32)
    return o[:, :LANES] / (o[:, LANES:] + jnp.concatenate(sink_terms, axis=0))


def _window_mask(rows, q_index, min_key):
    kj = lax.broadcasted_iota(jnp.int32, (rows, KEYS), 1)
    return (kj >= q_index) & (kj <= q_index + WINDOW) & (kj >= min_key)


def _attn_prompt_kernel(sink_ref, q_ref, kvc_ref, kvp_ref, o_ref):
    min_key = jnp.where(pl.program_id(1) == 0, BLOCK, 0)
    kv = jnp.concatenate([kvp_ref[...], kvc_ref[...]], axis=0)
    qi = lax.broadcasted_iota(jnp.int32, (BLOCK, KEYS), 0)
    mask = _window_mask(BLOCK, qi, min_key)
    for h in range(N_KV_HEADS):
        k_bd, v_bd = _kv_block_diag(kv[:, :KV_DIM], kv[:, KV_DIM:], h)
        tiles = range(h * TILES_PER_KV, (h + 1) * TILES_PER_KV)
        q = jnp.concatenate([q_ref[:, t * LANES:(t + 1) * LANES] for t in tiles], axis=0)
        sinks = [(sink_ref[HEADS_PER_TILE * t], sink_ref[HEADS_PER_TILE * t + 1]) for t in tiles]
        o = _attend(q, k_bd, v_bd, mask, sinks).astype(BF16)
        for n, t in enumerate(tiles):
            o_ref[:, t * LANES:(t + 1) * LANES] = o[n * BLOCK:(n + 1) * BLOCK]


def _attn_prompt(q, kv, sinks):
    nb = SEQ // BLOCK
    return pl.pallas_call(
        _attn_prompt_kernel,
        grid=(BATCH, nb),
        in_specs=[
            pl.BlockSpec(memory_space=pltpu.SMEM),
            pl.BlockSpec((BLOCK, D_MODEL), lambda b, n: (b * nb + n, 0)),
            pl.BlockSpec((BLOCK, 2 * KV_DIM), lambda b, n: (b * nb + n, 0)),
            pl.BlockSpec((BLOCK, 2 * KV_DIM), lambda b, n: (b * nb + jnp.maximum(n - 1, 0), 0)),
        ],
        out_specs=pl.BlockSpec((BLOCK, D_MODEL), lambda b, n: (b * nb + n, 0)),
        out_shape=jax.ShapeDtypeStruct((M_PROMPT, D_MODEL), BF16),
        compiler_params=_params("arbitrary", "arbitrary"),
        name="attn_prompt",
    )(sinks, q, kv, kv)


SAMPLE_Q_ROWS = Q_TILES * DEC_SEQ
ROWS_PER_KV = TILES_PER_KV * DEC_SEQ


SAMPLE_BB = 4


def _attn_sample_kernel(sink_ref, q_ref, ck_ref, cv_ref, kn_ref, vn_ref, o_ref, kbuf, vbuf):
    row = lax.broadcasted_iota(jnp.int32, (ROWS_PER_KV, KEYS), 0)
    mask = _window_mask(ROWS_PER_KV, row % DEC_SEQ, 0)
    tile_of_row = lax.broadcasted_iota(jnp.int32, (ROWS_PER_KV, 1), 0) // DEC_SEQ
    sink_cols = []
    for h in range(N_KV_HEADS):
        sink_a = jnp.zeros((ROWS_PER_KV, 1), F32)
        sink_b = jnp.zeros((ROWS_PER_KV, 1), F32)
        for t in range(TILES_PER_KV):
            tile = h * TILES_PER_KV + t
            sink_a = jnp.where(tile_of_row == t, sink_ref[HEADS_PER_TILE * tile], sink_a)
            sink_b = jnp.where(tile_of_row == t, sink_ref[HEADS_PER_TILE * tile + 1], sink_b)
        sink_cols.append((sink_a, sink_b))
    pad = jnp.zeros((KEYS - WINDOW, KV_DIM), F32)
    for b in range(SAMPLE_BB):
        kbuf[b, WINDOW:, :] = pad
        vbuf[b, WINDOW:, :] = pad
        kbuf[b, :WINDOW, :] = ck_ref[b]
        vbuf[b, :WINDOW, :] = cv_ref[b]
        kbuf[b, WINDOW:WINDOW + DEC_SEQ, :] = kn_ref[b]
        vbuf[b, WINDOW:WINDOW + DEC_SEQ, :] = vn_ref[b]
        k_all = kbuf[b]
        v_all = vbuf[b]
        for h in range(N_KV_HEADS):
            k_bd, v_bd = _kv_block_diag(k_all, v_all, h)
            rows = slice(h * ROWS_PER_KV, (h + 1) * ROWS_PER_KV)
            o_ref[b, rows, :] = _attend(q_ref[b, rows, :], k_bd, v_bd, mask, [sink_cols[h]]).astype(BF16)


def _attn_sample(q_s, cache_k, cache_v, k_new, v_new, sinks):
    per_b = lambda b: (b, 0, 0)
    return pl.pallas_call(
        _attn_sample_kernel,
        grid=(DEC_BATCH // SAMPLE_BB,),
        in_specs=[
            pl.BlockSpec(memory_space=pltpu.SMEM),
            pl.BlockSpec((SAMPLE_BB, SAMPLE_Q_ROWS, LANES), per_b),
            pl.BlockSpec((SAMPLE_BB, WINDOW, KV_DIM), per_b),
            pl.BlockSpec((SAMPLE_BB, WINDOW, KV_DIM), per_b),
            pl.BlockSpec((SAMPLE_BB, DEC_SEQ, KV_DIM), per_b),
            pl.BlockSpec((SAMPLE_BB, DEC_SEQ, KV_DIM), per_b),
        ],
        out_specs=pl.BlockSpec((SAMPLE_BB, SAMPLE_Q_ROWS, LANES), per_b),
        out_shape=jax.ShapeDtypeStruct((DEC_BATCH, SAMPLE_Q_ROWS, LANES), BF16),
        scratch_shapes=[pltpu.VMEM((SAMPLE_BB, KEYS, KV_DIM), F32), pltpu.VMEM((SAMPLE_BB, KEYS, KV_DIM), F32)],
        compiler_params=_params("arbitrary"),
        name="attn_sample",
    )(sinks, q_s, cache_k, cache_v, k_new, v_new)


def _prompt_tail_slices():
    out = []
    for b in range(BATCH):
        end = (b + 1) * SEQ
        tile = (end - 1) // TM
        out.append((tile, end - tile * TM - SUBLANES))
    return out


def _conv_mixer_kernel(x_ref, g_ref, wb_ref, wc_ref, wh_ref, cw_ref, p1_ref, p2_ref,
                       a_ref, zp_ref, zs_ref, h_ref, carry_ref):
    i, j = pl.program_id(0), pl.program_id(1)

    @pl.when(j == 0)
    def _():
        h_ref[...] = _rms(x_ref[...], g_ref[...]).astype(BF16)

    @pl.when(i == 0)
    def _():
        carry_ref[j] = jnp.zeros(carry_ref.shape[1:], F32)

    h = h_ref[...]
    c = jnp.dot(h, wc_ref[...], preferred_element_type=F32)
    hin = jnp.dot(h, wh_ref[...], preferred_element_type=F32)
    z = c * hin
    prev = carry_ref[j]
    carry_ref[j] = z[TM - SUBLANES:]

    row = lax.broadcasted_iota(jnp.int32, (TM, 1), 0)
    grow = row + i * TM
    is_sample = grow >= M_PROMPT
    t = jnp.where(is_sample, grow % DEC_SEQ, grow % SEQ)
    top = jnp.zeros((SAMPLE_ROW0, z.shape[1]), F32)
    pre1 = jnp.where(is_sample, jnp.concatenate([top, p1_ref[...]], axis=0), 0.0)
    pre2 = jnp.where(is_sample, jnp.concatenate([top, p2_ref[...]], axis=0), 0.0)
    last = prev[SUBLANES - 1:SUBLANES]
    zm1 = jnp.where(row == 0, last, pltpu.roll(z, 1, axis=0))
    zm2 = jnp.where(row == 0, prev[SUBLANES - 2:SUBLANES - 1],
                    jnp.where(row == 1, last, pltpu.roll(z, 2, axis=0)))
    zm1 = jnp.where(t < 1, pre1, zm1)
    zm2 = jnp.where(t < 2, pre2, zm2)
    y = cw_ref[0:1, :] * zm2 + cw_ref[1:2, :] * zm1 + cw_ref[2:3, :] * z
    b = jnp.dot(h, wb_ref[...], preferred_element_type=F32)
    a_ref[...] = (b * y).astype(BF16)

    zs_ref[...] = z[SAMPLE_ROW0:]
    tails = _prompt_tail_slices()
    zp = z[tails[0][1]:tails[0][1] + SUBLANES]
    for tile, lo in tails[1:]:
        zp = jnp.where(i == tile, z[lo:lo + SUBLANES], zp)
    zp_ref[...] = zp


def _conv_mixer(x, g, w_in, w_conv, p1, p2):
    nj = D_MODEL // CONV_TN
    col = lambda k: (lambda i, j: (0, j + k * nj))
    return pl.pallas_call(
        _conv_mixer_kernel,
        grid=(N_ROW_TILES, nj),
        in_specs=[
            pl.BlockSpec((TM, D_MODEL), lambda i, j: (i, 0)),
            _gain_spec(g),
            pl.BlockSpec((D_MODEL, CONV_TN), col(0)),
            pl.BlockSpec((D_MODEL, CONV_TN), col(1)),
            pl.BlockSpec((D_MODEL, CONV_TN), col(2)),
            pl.BlockSpec((CONV_WIDTH, CONV_TN), lambda i, j: (0, j)),
            pl.BlockSpec((M_SAMPLE, CONV_TN), lambda i, j: (0, j)),
            pl.BlockSpec((M_SAMPLE, CONV_TN), lambda i, j: (0, j)),
        ],
        out_specs=[
            pl.BlockSpec((TM, CONV_TN), lambda i, j: (i, j)),
            pl.BlockSpec((SUBLANES, CONV_TN), lambda i, j: (i, j)),
            pl.BlockSpec((M_SAMPLE, CONV_TN), lambda i, j: (0, jnp.where(i == N_ROW_TILES - 1, j, 0))),
        ],
        out_shape=[
            jax.ShapeDtypeStruct((M_ALL, D_MODEL), BF16),
            jax.ShapeDtypeStruct((N_ROW_TILES * SUBLANES, D_MODEL), F32),
            jax.ShapeDtypeStruct((M_SAMPLE, D_MODEL), F32),
        ],
        scratch_shapes=[pltpu.VMEM((TM, D_MODEL), BF16), pltpu.VMEM((nj, SUBLANES, CONV_TN), F32)],
        compiler_params=_params("arbitrary", "arbitrary"),
        name="conv_mixer",
    )(x, g[0], w_in, w_in, w_in, w_conv, p1, p2)


def _sgu_mixer_kernel(x_ref, g_ref, w_ref, lg_ref, lb_ref, ws_ref, bst_ref, wss_ref, bsts_ref,
                      a_ref, vln_ref, h_ref, mixed_ref):
    i, j = pl.program_id(0), pl.program_id(1)
    n_chunks = TM // CHUNK

    @pl.when(j == 0)
    def _():
        h = _rms(x_ref[...], g_ref[...]).astype(BF16)
        h_ref[...] = h
        v = jax.nn.gelu(jnp.dot(h, w_ref[...], preferred_element_type=F32))
        mu = jnp.mean(v, axis=-1, keepdims=True)
        xc = v - mu
        vln = xc * lax.rsqrt(jnp.mean(xc * xc, axis=-1, keepdims=True) + LN_EPS) * lg_ref[...] + lb_ref[...]
        vln_ref[...] = vln[SAMPLE_ROW0:]
        vb = vln.astype(BF16)
        r = lax.broadcasted_iota(jnp.int32, (CHUNK, CHUNK), 0)
        c = lax.broadcasted_iota(jnp.int32, (CHUNK, CHUNK), 1)
        causal = r >= c
        last_tile = i == N_ROW_TILES - 1
        for grp in range(SGU_GROUPS):
            cols = slice(grp * SGU_GROUP_DIM, (grp + 1) * SGU_GROUP_DIM)
            w_p = jnp.where(causal, ws_ref[grp], 0.0)
            w_s = jnp.where(last_tile, jnp.where(causal, wss_ref[grp], 0.0), w_p)
            b_p = bst_ref[:, grp:grp + 1]
            b_s = jnp.where(last_tile, bsts_ref[:, grp:grp + 1], b_p)
            for ch in range(n_chunks):
                rows = slice(ch * CHUNK, (ch + 1) * CHUNK)
                sample_chunk = ch == n_chunks - 1
                w = (w_s if sample_chunk else w_p).astype(BF16)
                bias = b_s if sample_chunk else b_p
                mixed_ref[rows, cols] = jnp.dot(w, vb[rows, cols], preferred_element_type=F32) + bias

    @pl.when(j == 1)
    def _():
        u = jax.nn.gelu(jnp.dot(h_ref[...], w_ref[...], preferred_element_type=F32))
        a_ref[...] = (u * mixed_ref[...]).astype(BF16)


def _sgu_mixer(x, g, w_in, ln_g, ln_b, ws, bst, ws_s, bst_s):
    fixed2 = lambda i, j: (0, 0)
    return pl.pallas_call(
        _sgu_mixer_kernel,
        grid=(N_ROW_TILES, 2),
        in_specs=[
            pl.BlockSpec((TM, D_MODEL), lambda i, j: (i, 0)),
            _gain_spec(g),
            pl.BlockSpec((D_MODEL, SGU_WIDTH), lambda i, j: (0, 1 - j)),
            pl.BlockSpec((1, SGU_WIDTH), fixed2),
            pl.BlockSpec((1, SGU_WIDTH), fixed2),
            pl.BlockSpec((SGU_GROUPS, CHUNK, CHUNK), lambda i, j: (0, 0, 0)),
            pl.BlockSpec((CHUNK, SGU_GROUPS), fixed2),
            pl.BlockSpec((SGU_GROUPS, CHUNK, CHUNK), lambda i, j: (0, 0, 0)),
            pl.BlockSpec((CHUNK, SGU_GROUPS), fixed2),
        ],
        out_specs=[
            pl.BlockSpec((TM, SGU_WIDTH), lambda i, j: (i, 0)),
            pl.BlockSpec((M_SAMPLE, SGU_WIDTH), fixed2),
        ],
        out_shape=[
            jax.ShapeDtypeStruct((M_ALL, SGU_WIDTH), BF16),
            jax.ShapeDtypeStruct((M_SAMPLE, SGU_WIDTH), F32),
        ],
        scratch_shapes=[pltpu.VMEM((TM, D_MODEL), BF16), pltpu.VMEM((TM, SGU_WIDTH), F32)],
        compiler_params=_params("arbitrary", "arbitrary"),
        name="sgu_mixer",
    )(x, g[0], w_in, ln_g, ln_b, ws, bst, ws_s, bst_s)


def _attention_layer(x, g_pre, w_qkv, sinks, cache_k, cache_v, tables):
    q, kv = _attn_proj(x, g_pre, w_qkv, tables)
    o_p = _attn_prompt(q, kv, sinks)

    q_s = q[M_PROMPT:].reshape(DEC_BATCH, DEC_SEQ, Q_TILES, LANES).transpose(0, 2, 1, 3)
    q_s = q_s.reshape(DEC_BATCH, SAMPLE_Q_ROWS, LANES)
    kv_s = kv[M_PROMPT:].reshape(DEC_BATCH, DEC_SEQ, 2 * KV_DIM)
    k_new, v_new = kv_s[..., :KV_DIM], kv_s[..., KV_DIM:]
    ck = cache_k.reshape(DEC_BATCH, WINDOW, KV_DIM)
    cv = cache_v.reshape(DEC_BATCH, WINDOW, KV_DIM)
    o_s = _attn_sample(q_s, ck, cv, k_new, v_new, sinks)
    o_s = o_s.reshape(DEC_BATCH, Q_TILES, DEC_SEQ, LANES).transpose(0, 2, 1, 3).reshape(M_SAMPLE, D_MODEL)

    kv_p = kv[:M_PROMPT].reshape(BATCH, SEQ, 2 * KV_DIM)[:, SEQ - WINDOW:]
    kv_shape = (BATCH, WINDOW, N_KV_HEADS, HEAD_DIM)
    new_kp = kv_p[..., :KV_DIM].reshape(kv_shape)
    new_vp = kv_p[..., KV_DIM:].reshape(kv_shape)
    kv_shape = (DEC_BATCH, WINDOW, N_KV_HEADS, HEAD_DIM)
    new_ks = jnp.concatenate([ck[:, DEC_SEQ:], k_new], axis=1).reshape(kv_shape)
    new_vs = jnp.concatenate([cv[:, DEC_SEQ:], v_new], axis=1).reshape(kv_shape)
    return (o_p, o_s), (new_kp, new_vp, new_ks, new_vs)


def _conv_layer(x, g_pre, w_in, w_conv, state):
    zeros = jnp.zeros((DEC_BATCH, DEC_SEQ - 1, D_MODEL), F32)
    p1 = jnp.concatenate([state[:, 1:2], zeros], axis=1).reshape(M_SAMPLE, D_MODEL)
    p2 = jnp.concatenate([state[:, 0:1], state[:, 1:2], zeros[:, 1:]], axis=1).reshape(M_SAMPLE, D_MODEL)
    a, z_p, z_s = _conv_mixer(x, g_pre, w_in, w_conv, p1, p2)
    keep = CONV_WIDTH - 1
    new_cp = jnp.stack([z_p[(tile + 1) * SUBLANES - keep:(tile + 1) * SUBLANES]
                        for tile, _ in _prompt_tail_slices()])
    new_cs = z_s.reshape(DEC_BATCH, DEC_SEQ, D_MODEL)[:, DEC_SEQ - keep:]
    return a, (new_cp, new_cs)


def _sgu_layer(x, g_pre, w_in, ln_g, ln_b, w_s, b_s):
    eye = jnp.eye(DEC_BATCH, dtype=F32)
    ws_s = jnp.einsum("ab,gts->gatbs", eye, w_s[:, :DEC_SEQ, :DEC_SEQ]).reshape(SGU_GROUPS, M_SAMPLE, M_SAMPLE)
    bst_s = jnp.tile(b_s[:, :DEC_SEQ].T, (DEC_BATCH, 1))
    a, vln_s = _sgu_mixer(x, g_pre, w_in, ln_g, ln_b, w_s, b_s.T, ws_s, bst_s)
    return a, vln_s.reshape(DEC_BATCH, DEC_SEQ, SGU_WIDTH)


def kernel(x_prompt, x_sample, cache_k, cache_v, state_conv, norm_g, ffn_w_gu, ffn_w_down,
           attn_w_qkv, attn_w_o, attn_sinks, conv_w_in, conv_w, conv_w_out,
           sgu_w_in, sgu_ln_g, sgu_ln_b, sgu_w_s, sgu_b_s, sgu_w_out):
    x = (x_prompt.reshape(M_PROMPT, D_MODEL), x_sample.reshape(M_SAMPLE, D_MODEL))
    tables = _rope_tables()
    gains = norm_g.reshape(-1, 1, D_MODEL)
    gain = lambda i, k: (gains, i * norm_g.shape[1] + k)
    mixer_weights = ((attn_w_qkv, attn_w_o), (conv_w_in, conv_w_out), (sgu_w_in, sgu_w_out))
    w_gu, w_down = ffn_w_gu[0, 0].astype(BF16), ffn_w_down[0, 0].astype(BF16)
    kp_l, vp_l, ks_l, vs_l, cp_l, cs_l, sv_l = [], [], [], [], [], [], []
    for i in range(DEPTH):
        kind, j = i % N_MIXERS, i // N_MIXERS
        w_in_f32, w_out_f32 = mixer_weights[kind]
        x, (w_in, w_out, w_gu, w_down) = _ffn(
            x, gain(i, 0), gain(i, 1), w_gu, w_down,
            casts=((w_in_f32, (j,)), (w_out_f32, (j,)), (ffn_w_gu, (i, 1)), (ffn_w_down, (i, 1))))
        if kind == 0:
            a, (kp, vp, ks, vs) = _attention_layer(x, gain(i, 2), w_in, attn_sinks[j],
                                                   cache_k[j], cache_v[j], tables)
            kp_l.append(kp); vp_l.append(vp); ks_l.append(ks); vs_l.append(vs)
        elif kind == 1:
            a, (cp, cs) = _conv_layer(x, gain(i, 2), w_in, conv_w[j], state_conv[j])
            cp_l.append(cp); cs_l.append(cs)
        else:
            a, sv = _sgu_layer(x, gain(i, 2), w_in, sgu_ln_g[j].reshape(1, SGU_WIDTH),
                               sgu_ln_b[j].reshape(1, SGU_WIDTH), sgu_w_s[j], sgu_b_s[j])
            sv_l.append(sv)
        x = _outproj(x, a, gain(i, 3), w_out)
        last = i + 1 == DEPTH
        casts = () if last else ((ffn_w_gu, (i + 1, 0)), (ffn_w_down, (i + 1, 0)))
        x, next_ffn = _ffn(x, gain(i, 4), gain(i, 5), w_gu, w_down, casts=casts, split_out=last)
        if next_ffn:
            w_gu, w_down = next_ffn
    y_prompt = x[0].reshape(BATCH, SEQ, D_MODEL)
    y_sample = x[1].reshape(DEC_BATCH, DEC_SEQ, D_MODEL)
    return (y_prompt, y_sample, jnp.stack(kp_l), jnp.stack(vp_l), jnp.stack(ks_l), jnp.stack(vs_l),
            jnp.stack(cp_l), jnp.stack(cs_l), jnp.stack(sv_l))
```

```python
import functools

import jax
import jax.numpy as jnp
from jax import lax
from jax.experimental import pallas as pl
from jax.experimental.pallas import tpu as pltpu

D_MODEL = 2048
BATCH = 2
SEQ = 4096
DEPTH = 4
DEC_BATCH = 32
DEC_SEQ = 4
PAST_LEN = 16384
HEAD_DIM = 64
N_HEADS = D_MODEL // HEAD_DIM
N_KV_HEADS = N_HEADS // 8
ROT_DIM = HEAD_DIM // 4
ROPE_THETA = 500000.0
WINDOW = 128
BLOCK = 128
CONV_WIDTH = 3
SGU_WIDTH = D_MODEL
SGU_GROUPS = 8
SGU_GROUP_DIM = SGU_WIDTH // SGU_GROUPS
CHUNK = 128
D_FF = 5632
N_MIXERS = 3
RMS_EPS = 1e-6
LN_EPS = 1e-5
NEG_INF = -1e30
LOG2E = 1.4426950408889634

M_PROMPT = BATCH * SEQ
M_SAMPLE = DEC_BATCH * DEC_SEQ
M_ALL = M_PROMPT + M_SAMPLE
KV_DIM = N_KV_HEADS * HEAD_DIM
QKV_DIM = D_MODEL + 2 * KV_DIM

LANES = 128
SUBLANES = 8
BF16_SUBLANES = 16
MXU_COLS = 256
HEADS_PER_TILE = LANES // HEAD_DIM
Q_TILES = D_MODEL // LANES
TILES_PER_KV = Q_TILES // N_KV_HEADS
KEYS = 2 * BLOCK

TM = 640
N_ROW_TILES = M_ALL // TM
SAMPLE_ROW0 = M_PROMPT - (N_ROW_TILES - 1) * TM
TF = 512
ROW_CHUNK = 16
CONV_TN = 512
CAST_STEPS = 2
VMEM_LIMIT = 58 * 1024 * 1024

assert M_ALL % TM == 0 and SAMPLE_ROW0 + M_SAMPLE == TM and M_SAMPLE == CHUNK and TM % CHUNK == 0
assert M_PROMPT % DEC_SEQ == 0 and SEQ % CHUNK == 0
assert TM % ROW_CHUNK == 0 and SAMPLE_ROW0 % ROW_CHUNK == 0 and ROW_CHUNK % BF16_SUBLANES == 0

F32 = jnp.float32
BF16 = jnp.bfloat16


def _rms(x, g):
    return x * lax.rsqrt(jnp.mean(x * x, axis=-1, keepdims=True) + RMS_EPS) * g


def _params(*sem):
    return pltpu.CompilerParams(dimension_semantics=sem, vmem_limit_bytes=VMEM_LIMIT)


def _gain_spec(gain):
    return pl.BlockSpec((None, 1, D_MODEL), lambda *_: (gain[1], 0, 0))


def _wspec(lead, block, idx):
    return pl.BlockSpec((None,) * len(lead) + block, lambda *g: lead + idx(*g))


def _row_chunks(rows):
    return [slice(c * ROW_CHUNK, (c + 1) * ROW_CHUNK) for c in range(rows // ROW_CHUNK)]


def _x_reader(x_ref, xs_ref, n_tiles, sample_row0):
    def read(rows):
        x = x_ref[rows, :]
        if xs_ref is not None and rows.start >= sample_row0:
            tail = xs_ref[rows.start - sample_row0:rows.stop - sample_row0, :]
            x = jnp.where(pl.program_id(0) == n_tiles - 1, tail, x)
        return x
    return read


def _ffn_kernel(*refs, nj, n_cast, split_in, split_out):
    refs = list(refs)
    x_ref = refs.pop(0)
    xs_ref = refs.pop(0) if split_in else None
    gpre_ref, gpost_ref, wg_ref, wu_ref, wd_ref = refs[:5]
    cast_src = refs[5:5 + n_cast]
    refs = refs[5 + n_cast:]
    o_ref = refs.pop(0)
    os_ref = refs.pop(0) if split_out else None
    cast_dst = refs[:n_cast]
    h_ref, acc_ref = refs[n_cast:]
    i, j = pl.program_id(0), pl.program_id(1)
    read_x = _x_reader(x_ref, xs_ref, N_ROW_TILES, SAMPLE_ROW0)

    @pl.when((i == 0) & (j == 0))
    def _():
        acc_ref[...] = jnp.zeros_like(acc_ref)

    @pl.when(j == 0)
    def _():
        for rows in _row_chunks(TM):
            h_ref[rows, :] = _rms(read_x(rows), gpre_ref[...]).astype(BF16)

    h = h_ref[...]
    g = jnp.dot(h, wg_ref[...], preferred_element_type=F32)
    part = (i * nj + j) % CAST_STEPS
    for src, dst in zip(cast_src, cast_dst):
        rps = src.shape[0] // CAST_STEPS
        rows = pl.ds(pl.multiple_of(part * rps, rps), rps)
        dst[rows, :] = src[rows, :].astype(BF16)
    u = jnp.dot(h, wu_ref[...], preferred_element_type=F32)
    a = (g * jax.nn.sigmoid(g) * u).astype(BF16)
    d = jnp.dot(a, wd_ref[...], preferred_element_type=F32)
    acc_ref[...] = jnp.where(j == 0, 0.0, acc_ref[...]) + d

    @pl.when(j == nj - 1)
    def _():
        gain = 0.5 * gpost_ref[...]
        for rows in _row_chunks(TM):
            y = read_x(rows) + _rms(acc_ref[rows, :], gain)
            o_ref[rows, :] = y
            if os_ref is not None and rows.start >= SAMPLE_ROW0:
                os_ref[rows.start - SAMPLE_ROW0:rows.stop - SAMPLE_ROW0, :] = y


def _cast_rows_per_step(rows, steps):
    rps = BF16_SUBLANES
    while rows % rps or rows // rps > steps:
        rps += BF16_SUBLANES
    return rps


def _ffn(x, g_pre, g_post, w_gu, w_down, casts=(), split_out=False):
    ni, nj = N_ROW_TILES, D_FF // TF
    row_tile = pl.BlockSpec((TM, D_MODEL), lambda i, j: (i, 0))
    sample_rows = pl.BlockSpec((M_SAMPLE, D_MODEL), lambda i, j: (0, 0))
    split_in = isinstance(x, tuple)
    x_args, x_specs = (list(x), [row_tile, sample_rows]) if split_in else ([x], [row_tile])
    if split_out:
        o_specs = [row_tile, sample_rows]
        o_shapes = [jax.ShapeDtypeStruct((M_PROMPT, D_MODEL), F32), jax.ShapeDtypeStruct((M_SAMPLE, D_MODEL), F32)]
    else:
        o_specs, o_shapes = [row_tile], [jax.ShapeDtypeStruct((M_ALL, D_MODEL), F32)]
    cast_in_specs, cast_out_specs, cast_shapes, cast_args = [], [], [], []
    for w, lead in casts:
        rows, cols = w.shape[-2:]
        block_rows = CAST_STEPS * _cast_rows_per_step(rows, ni * nj)
        assert rows % block_rows == 0
        slab = lambda i, j, last=rows // block_rows - 1: (jnp.minimum((i * nj + j) // CAST_STEPS, last), 0)
        cast_in_specs.append(_wspec(lead, (block_rows, cols), slab))
        cast_out_specs.append(pl.BlockSpec((block_rows, cols), slab))
        cast_shapes.append(jax.ShapeDtypeStruct((rows, cols), BF16))
        cast_args.append(w)
    out = pl.pallas_call(
        functools.partial(_ffn_kernel, nj=nj, n_cast=len(casts), split_in=split_in, split_out=split_out),
        grid=(ni, nj),
        in_specs=x_specs + [
            _gain_spec(g_pre),
            _gain_spec(g_post),
            pl.BlockSpec((D_MODEL, TF), lambda i, j: (0, j)),
            pl.BlockSpec((D_MODEL, TF), lambda i, j: (0, j + nj)),
            pl.BlockSpec((TF, D_MODEL), lambda i, j: (j, 0)),
        ] + cast_in_specs,
        out_specs=o_specs + cast_out_specs,
        out_shape=o_shapes + cast_shapes,
        scratch_shapes=[pltpu.VMEM((TM, D_MODEL), BF16), pltpu.VMEM((TM, D_MODEL), F32)],
        compiler_params=_params("arbitrary", "arbitrary"),
        name="ffn",
    )(*x_args, g_pre[0], g_post[0], w_gu, w_gu, w_down, *cast_args)
    n_x = len(o_specs)
    return (tuple(out[:n_x]) if split_out else out[0]), out[n_x:]


def _outproj_kernel(*refs, split_a):
    x_ref, a_ref = refs[:2]
    as_ref = refs[2] if split_a else None
    g_ref, w_ref, o_ref, y_ref = refs[-4:]
    a = a_ref[...]
    if as_ref is not None:
        tail = jnp.where(pl.program_id(0) == N_ROW_TILES - 1, as_ref[...], a_ref[SAMPLE_ROW0:, :])
        a = jnp.concatenate([a_ref[:SAMPLE_ROW0, :], tail], axis=0)
    y_ref[...] = jnp.dot(a, w_ref[...], preferred_element_type=F32)
    for c in range(TM // ROW_CHUNK):
        rows = slice(c * ROW_CHUNK, (c + 1) * ROW_CHUNK)
        o_ref[rows, :] = x_ref[rows, :] + _rms(y_ref[rows, :], g_ref[...])


def _outproj(x, a, g, w):
    row_tile = pl.BlockSpec((TM, D_MODEL), lambda i: (i, 0))
    split_a = isinstance(a, tuple)
    a_args = list(a) if split_a else [a]
    a_specs = [row_tile] + ([pl.BlockSpec((M_SAMPLE, D_MODEL), lambda i: (0, 0))] if split_a else [])
    return pl.pallas_call(
        functools.partial(_outproj_kernel, split_a=split_a),
        grid=(N_ROW_TILES,),
        in_specs=[row_tile] + a_specs + [
            _gain_spec(g),
            pl.BlockSpec((D_MODEL, D_MODEL), lambda i: (0, 0)),
        ],
        out_specs=row_tile,
        out_shape=jax.ShapeDtypeStruct((M_ALL, D_MODEL), F32),
        scratch_shapes=[pltpu.VMEM((TM, D_MODEL), F32)],
        compiler_params=_params("arbitrary"),
        name="outproj",
    )(x, *a_args, g[0], w)


def _rope_tables():
    inv = jnp.power(jnp.float32(ROPE_THETA), -jnp.arange(0, ROT_DIM, 2, dtype=jnp.float32) / ROT_DIM)
    pos = jnp.concatenate([
        jnp.tile(jnp.arange(SEQ), BATCH),
        jnp.tile(PAST_LEN + jnp.arange(DEC_SEQ), DEC_BATCH),
    ])
    ang = pos.astype(jnp.float32)[:, None] * inv[None, :]
    c, s = jnp.cos(ang), jnp.sin(ang)
    pad = jnp.zeros((M_ALL, HEAD_DIM - ROT_DIM), F32)
    zero = jnp.zeros_like(s)
    cos_h = jnp.concatenate([c, c, pad + 1.0], axis=1)
    sa_h = jnp.concatenate([-s, zero, pad], axis=1)
    sb_h = jnp.concatenate([zero, s, pad], axis=1)
    rep = lambda t: jnp.tile(t, (1, HEADS_PER_TILE))
    return rep(cos_h), rep(sa_h), rep(sb_h)


def _attn_proj_kernel(x_ref, g_ref, w_ref, cos_ref, sa_ref, sb_ref, q_ref, kv_ref):
    h = _rms(x_ref[...], g_ref[...]).astype(BF16)
    cos, sa, sb = cos_ref[...], sa_ref[...], sb_ref[...]
    half = ROT_DIM // 2
    for c in range(QKV_DIM // MXU_COLS):
        y2 = jnp.dot(h, w_ref[:, c * MXU_COLS:(c + 1) * MXU_COLS], preferred_element_type=F32)
        for part in range(MXU_COLS // LANES):
            t = c * (MXU_COLS // LANES) + part
            y = y2[:, part * LANES:(part + 1) * LANES]
            if t < Q_TILES + KV_DIM // LANES:
                y = y * cos + pltpu.roll(y, LANES - half, axis=1) * sa + pltpu.roll(y, half, axis=1) * sb
            if t < Q_TILES:
                q_ref[:, t * LANES:(t + 1) * LANES] = (y * (HEAD_DIM ** -0.5 * LOG2E)).astype(BF16)
            else:
                kv_ref[:, (t - Q_TILES) * LANES:(t - Q_TILES + 1) * LANES] = y


def _attn_proj(x, g, w, tables):
    row = lambda i: (i, 0)
    fixed = lambda i: (0, 0)
    return pl.pallas_call(
        _attn_proj_kernel,
        grid=(N_ROW_TILES,),
        in_specs=[
            pl.BlockSpec((TM, D_MODEL), row),
            _gain_spec(g),
            pl.BlockSpec((D_MODEL, QKV_DIM), fixed),
            pl.BlockSpec((TM, LANES), row),
            pl.BlockSpec((TM, LANES), row),
            pl.BlockSpec((TM, LANES), row),
        ],
        out_specs=[pl.BlockSpec((TM, D_MODEL), row), pl.BlockSpec((TM, 2 * KV_DIM), row)],
        out_shape=[jax.ShapeDtypeStruct((M_ALL, D_MODEL), BF16),
                   jax.ShapeDtypeStruct((M_ALL, 2 * KV_DIM), F32)],
        compiler_params=_params("arbitrary"),
        name="attn_proj",
    )(x, g[0], w, *tables)


def _both_halves(tile, first_half):
    lane = lax.broadcasted_iota(jnp.int32, tile.shape, 1)
    rolled = pltpu.roll(tile, HEAD_DIM, axis=1)
    keep = (lane < HEAD_DIM) if first_half else (lane >= HEAD_DIM)
    return jnp.where(keep, tile, rolled)


def _block_diag(rep):
    lane = lax.broadcasted_iota(jnp.int32, rep.shape, 1)
    top = jnp.where(lane < HEAD_DIM, rep, 0.0)
    bot = jnp.where(lane >= HEAD_DIM, rep, 0.0)
    return jnp.concatenate([top, bot], axis=0).astype(BF16)


def _kv_block_diag(k_all, v_all, h):
    lo = (h // HEADS_PER_TILE) * LANES
    first_half = h % HEADS_PER_TILE == 0
    return (_block_diag(_both_halves(k_all[:, lo:lo + LANES], first_half)),
            _block_diag(_both_halves(v_all[:, lo:lo + LANES], first_half)))


def _attend(q, k_bd, v_bd, mask, sinks):
    rows = mask.shape[0]
    s_all = lax.dot_general(q, k_bd, (((1,), (1,)), ((), ())), preferred_element_type=F32)
    lane = lax.broadcasted_iota(jnp.int32, (rows, LANES), 1)
    ps, sink_terms = [], []
    for t, (sink_a, sink_b) in enumerate(sinks):
        s = s_all[t * rows:(t + 1) * rows]
        sa = jnp.where(mask, s[:, :KEYS], NEG_INF)
        sb = jnp.where(mask, s[:, KEYS:], NEG_INF)
        ma = jnp.maximum(jnp.max(sa, axis=-1, keepdims=True), sink_a * LOG2E)
        mb = jnp.maximum(jnp.max(sb, axis=-1, keepdims=True), sink_b * LOG2E)
        ps.append(jnp.concatenate([jnp.exp2(sa - ma), jnp.exp2(sb - mb)], axis=1).astype(BF16))
        sink_terms.append(jnp.where(lane < HEAD_DIM, jnp.exp2(sink_a * LOG2E - ma), jnp.exp2(sink_b * LOG2E - mb)))
    key = lax.broadcasted_iota(jnp.int32, v_bd.shape, 0)
    lane_v = lax.broadcasted_iota(jnp.int32, v_bd.shape, 1)
    ones_bd = jnp.where((key < KEYS) == (lane_v < HEAD_DIM), 1.0, 0.0).astype(BF16)
    o = jnp.dot(jnp.concatenate(ps, axis=0), jnp.concatenate([v_bd, ones_bd], axis=1),
                preferred_element_type=F32)
    return o[:, :LANES] / (o[:, LANES:] + jnp.concatenate(sink_terms, axis=0))


def _window_mask(rows, q_index, min_key):
    kj = lax.broadcasted_iota(jnp.int32, (rows, KEYS), 1)
    return (kj >= q_index) & (kj <= q_index + WINDOW) & (kj >= min_key)


def _attn_prompt_kernel(sink_ref, q_ref, kvc_ref, kvp_ref, o_ref, kv_tail_ref):
    kv_tail_ref[...] = kvc_ref[...]
    min_key = jnp.where(pl.program_id(1) == 0, BLOCK, 0)
    kv = jnp.concatenate([kvp_ref[...], kvc_ref[...]], axis=0)
    qi = lax.broadcasted_iota(jnp.int32, (BLOCK, KEYS), 0)
    mask = _window_mask(BLOCK, qi, min_key)
    for h in range(N_KV_HEADS):
        k_bd, v_bd = _kv_block_diag(kv[:, :KV_DIM], kv[:, KV_DIM:], h)
        tiles = range(h * TILES_PER_KV, (h + 1) * TILES_PER_KV)
        q = jnp.concatenate([q_ref[:, t * LANES:(t + 1) * LANES] for t in tiles], axis=0)
        sinks = [(sink_ref[HEADS_PER_TILE * t], sink_ref[HEADS_PER_TILE * t + 1]) for t in tiles]
        o = _attend(q, k_bd, v_bd, mask, sinks).astype(BF16)
        for n, t in enumerate(tiles):
            o_ref[:, t * LANES:(t + 1) * LANES] = o[n * BLOCK:(n + 1) * BLOCK]


def _attn_prompt(q, kv, sinks):
    nb = SEQ // BLOCK
    return pl.pallas_call(
        _attn_prompt_kernel,
        grid=(BATCH, nb),
        in_specs=[
            pl.BlockSpec(memory_space=pltpu.SMEM),
            pl.BlockSpec((BLOCK, D_MODEL), lambda b, n: (b * nb + n, 0)),
            pl.BlockSpec((BLOCK, 2 * KV_DIM), lambda b, n: (b * nb + n, 0)),
            pl.BlockSpec((BLOCK, 2 * KV_DIM), lambda b, n: (b * nb + jnp.maximum(n - 1, 0), 0)),
        ],
        out_specs=[pl.BlockSpec((BLOCK, D_MODEL), lambda b, n: (b * nb + n, 0)),
                   pl.BlockSpec((None, BLOCK, 2 * KV_DIM), lambda b, n: (b, 0, 0))],
        out_shape=[jax.ShapeDtypeStruct((M_PROMPT, D_MODEL), BF16),
                   jax.ShapeDtypeStruct((BATCH, BLOCK, 2 * KV_DIM), F32)],
        compiler_params=_params("arbitrary", "arbitrary"),
        name="attn_prompt",
    )(sinks, q, kv, kv)


SAMPLE_Q_ROWS = Q_TILES * DEC_SEQ
ROWS_PER_KV = TILES_PER_KV * DEC_SEQ


SAMPLE_BB = 4


def _attn_sample_kernel(sink_ref, q_ref, ck_ref, cv_ref, kvn_ref, o_ref, ok_ref, ov_ref, kbuf, vbuf):
    row = lax.broadcasted_iota(jnp.int32, (ROWS_PER_KV, KEYS), 0)
    mask = _window_mask(ROWS_PER_KV, row % DEC_SEQ, 0)
    tile_of_row = lax.broadcasted_iota(jnp.int32, (ROWS_PER_KV, 1), 0) // DEC_SEQ
    sink_cols = []
    for h in range(N_KV_HEADS):
        sink_a = jnp.zeros((ROWS_PER_KV, 1), F32)
        sink_b = jnp.zeros((ROWS_PER_KV, 1), F32)
        for t in range(TILES_PER_KV):
            tile = h * TILES_PER_KV + t
            sink_a = jnp.where(tile_of_row == t, sink_ref[HEADS_PER_TILE * tile], sink_a)
            sink_b = jnp.where(tile_of_row == t, sink_ref[HEADS_PER_TILE * tile + 1], sink_b)
        sink_cols.append((sink_a, sink_b))
    pad = jnp.zeros((KEYS - WINDOW, KV_DIM), F32)
    for b in range(SAMPLE_BB):
        kbuf[b, WINDOW:, :] = pad
        vbuf[b, WINDOW:, :] = pad
        kbuf[b, :WINDOW, :] = ck_ref[b]
        vbuf[b, :WINDOW, :] = cv_ref[b]
        kbuf[b, WINDOW:WINDOW + DEC_SEQ, :] = kvn_ref[b, :, :KV_DIM]
        vbuf[b, WINDOW:WINDOW + DEC_SEQ, :] = kvn_ref[b, :, KV_DIM:]
        ok_ref[b] = kbuf[b, DEC_SEQ:DEC_SEQ + WINDOW, :]
        ov_ref[b] = vbuf[b, DEC_SEQ:DEC_SEQ + WINDOW, :]
        k_all = kbuf[b]
        v_all = vbuf[b]
        for h in range(N_KV_HEADS):
            k_bd, v_bd = _kv_block_diag(k_all, v_all, h)
            rows = slice(h * ROWS_PER_KV, (h + 1) * ROWS_PER_KV)
            o_ref[b, rows, :] = _attend(q_ref[b, rows, :], k_bd, v_bd, mask, [sink_cols[h]]).astype(BF16)


def _attn_sample(q_s, cache_k, cache_v, kv_new, sinks):
    per_b = lambda b: (b, 0, 0)
    cache_spec = pl.BlockSpec((SAMPLE_BB, WINDOW, KV_DIM), per_b)
    cache_shape = jax.ShapeDtypeStruct((DEC_BATCH, WINDOW, KV_DIM), F32)
    return pl.pallas_call(
        _attn_sample_kernel,
        grid=(DEC_BATCH // SAMPLE_BB,),
        in_specs=[
            pl.BlockSpec(memory_space=pltpu.SMEM),
            pl.BlockSpec((SAMPLE_BB, SAMPLE_Q_ROWS, LANES), per_b),
            cache_spec,
            cache_spec,
            pl.BlockSpec((SAMPLE_BB, DEC_SEQ, 2 * KV_DIM), per_b),
        ],
        out_specs=[pl.BlockSpec((SAMPLE_BB, SAMPLE_Q_ROWS, LANES), per_b), cache_spec, cache_spec],
        out_shape=[jax.ShapeDtypeStruct((DEC_BATCH, SAMPLE_Q_ROWS, LANES), BF16), cache_shape, cache_shape],
        scratch_shapes=[pltpu.VMEM((SAMPLE_BB, KEYS, KV_DIM), F32), pltpu.VMEM((SAMPLE_BB, KEYS, KV_DIM), F32)],
        compiler_params=_params("arbitrary"),
        name="attn_sample",
    )(sinks, q_s, cache_k, cache_v, kv_new)


def _prompt_tail_slices():
    out = []
    for b in range(BATCH):
        end = (b + 1) * SEQ
        tile = (end - 1) // TM
        out.append((tile, end - tile * TM - SUBLANES))
    return out


def _conv_mixer_kernel(x_ref, g_ref, wb_ref, wc_ref, wh_ref, cw_ref, p1_ref, p2_ref,
                       a_ref, zp_ref, zs_ref, h_ref, carry_ref):
    i, j = pl.program_id(0), pl.program_id(1)

    @pl.when(j == 0)
    def _():
        h_ref[...] = _rms(x_ref[...], g_ref[...]).astype(BF16)

    @pl.when(i == 0)
    def _():
        carry_ref[j] = jnp.zeros(carry_ref.shape[1:], F32)

    h = h_ref[...]
    c = jnp.dot(h, wc_ref[...], preferred_element_type=F32)
    hin = jnp.dot(h, wh_ref[...], preferred_element_type=F32)
    z = c * hin
    prev = carry_ref[j]
    carry_ref[j] = z[TM - SUBLANES:]

    row = lax.broadcasted_iota(jnp.int32, (TM, 1), 0)
    grow = row + i * TM
    is_sample = grow >= M_PROMPT
    t = jnp.where(is_sample, grow % DEC_SEQ, grow % SEQ)
    top = jnp.zeros((SAMPLE_ROW0, z.shape[1]), F32)
    pre1 = jnp.where(is_sample, jnp.concatenate([top, p1_ref[...]], axis=0), 0.0)
    pre2 = jnp.where(is_sample, jnp.concatenate([top, p2_ref[...]], axis=0), 0.0)
    last = prev[SUBLANES - 1:SUBLANES]
    zm1 = jnp.where(row == 0, last, pltpu.roll(z, 1, axis=0))
    zm2 = jnp.where(row == 0, prev[SUBLANES - 2:SUBLANES - 1],
                    jnp.where(row == 1, last, pltpu.roll(z, 2, axis=0)))
    zm1 = jnp.where(t < 1, pre1, zm1)
    zm2 = jnp.where(t < 2, pre2, zm2)
    y = cw_ref[0:1, :] * zm2 + cw_ref[1:2, :] * zm1 + cw_ref[2:3, :] * z
    b = jnp.dot(h, wb_ref[...], preferred_element_type=F32)
    a_ref[...] = (b * y).astype(BF16)

    zs_ref[...] = z[SAMPLE_ROW0:]
    tails = _prompt_tail_slices()
    zp = z[tails[0][1]:tails[0][1] + SUBLANES]
    for tile, lo in tails[1:]:
        zp = jnp.where(i == tile, z[lo:lo + SUBLANES], zp)
    zp_ref[...] = zp


def _conv_mixer(x, g, w_in, w_conv, p1, p2):
    nj = D_MODEL // CONV_TN
    col = lambda k: (lambda i, j: (0, j + k * nj))
    return pl.pallas_call(
        _conv_mixer_kernel,
        grid=(N_ROW_TILES, nj),
        in_specs=[
            pl.BlockSpec((TM, D_MODEL), lambda i, j: (i, 0)),
            _gain_spec(g),
            pl.BlockSpec((D_MODEL, CONV_TN), col(0)),
            pl.BlockSpec((D_MODEL, CONV_TN), col(1)),
            pl.BlockSpec((D_MODEL, CONV_TN), col(2)),
            pl.BlockSpec((CONV_WIDTH, CONV_TN), lambda i, j: (0, j)),
            pl.BlockSpec((M_SAMPLE, CONV_TN), lambda i, j: (0, j)),
            pl.BlockSpec((M_SAMPLE, CONV_TN), lambda i, j: (0, j)),
        ],
        out_specs=[
            pl.BlockSpec((TM, CONV_TN), lambda i, j: (i, j)),
            pl.BlockSpec((SUBLANES, CONV_TN), lambda i, j: (i, j)),
            pl.BlockSpec((M_SAMPLE, CONV_TN), lambda i, j: (0, jnp.where(i == N_ROW_TILES - 1, j, 0))),
        ],
        out_shape=[
            jax.ShapeDtypeStruct((M_ALL, D_MODEL), BF16),
            jax.ShapeDtypeStruct((N_ROW_TILES * SUBLANES, D_MODEL), F32),
            jax.ShapeDtypeStruct((M_SAMPLE, D_MODEL), F32),
        ],
        scratch_shapes=[pltpu.VMEM((TM, D_MODEL), BF16), pltpu.VMEM((nj, SUBLANES, CONV_TN), F32)],
        compiler_params=_params("arbitrary", "arbitrary"),
        name="conv_mixer",
    )(x, g[0], w_in, w_in, w_in, w_conv, p1, p2)


def _sgu_mixer_kernel(x_ref, g_ref, w_ref, lg_ref, lb_ref, ws_ref, bst_ref, wss_ref, bsts_ref,
                      a_ref, vln_ref, h_ref, mixed_ref):
    i, j = pl.program_id(0), pl.program_id(1)
    n_chunks = TM // CHUNK

    @pl.when(j == 0)
    def _():
        h = _rms(x_ref[...], g_ref[...]).astype(BF16)
        h_ref[...] = h
        v = jax.nn.gelu(jnp.dot(h, w_ref[...], preferred_element_type=F32))
        mu = jnp.mean(v, axis=-1, keepdims=True)
        xc = v - mu
        vln = xc * lax.rsqrt(jnp.mean(xc * xc, axis=-1, keepdims=True) + LN_EPS) * lg_ref[...] + lb_ref[...]
        vln_ref[...] = vln[SAMPLE_ROW0:]
        vb = vln.astype(BF16)
        r = lax.broadcasted_iota(jnp.int32, (CHUNK, CHUNK), 0)
        c = lax.broadcasted_iota(jnp.int32, (CHUNK, CHUNK), 1)
        causal = r >= c
        last_tile = i == N_ROW_TILES - 1
        for grp in range(SGU_GROUPS):
            cols = slice(grp * SGU_GROUP_DIM, (grp + 1) * SGU_GROUP_DIM)
            w_p = jnp.where(causal, ws_ref[grp], 0.0)
            w_s = jnp.where(last_tile, jnp.where(causal, wss_ref[grp], 0.0), w_p)
            b_p = bst_ref[:, grp:grp + 1]
            b_s = jnp.where(last_tile, bsts_ref[:, grp:grp + 1], b_p)
            for ch in range(n_chunks):
                rows = slice(ch * CHUNK, (ch + 1) * CHUNK)
                sample_chunk = ch == n_chunks - 1
                w = (w_s if sample_chunk else w_p).astype(BF16)
                bias = b_s if sample_chunk else b_p
                mixed_ref[rows, cols] = jnp.dot(w, vb[rows, cols], preferred_element_type=F32) + bias

    @pl.when(j == 1)
    def _():
        u = jax.nn.gelu(jnp.dot(h_ref[...], w_ref[...], preferred_element_type=F32))
        a_ref[...] = (u * mixed_ref[...]).astype(BF16)


def _sgu_mixer(x, g, w_in, ln_g, ln_b, ws, bst, ws_s, bst_s):
    fixed2 = lambda i, j: (0, 0)
    return pl.pallas_call(
        _sgu_mixer_kernel,
        grid=(N_ROW_TILES, 2),
        in_specs=[
            pl.BlockSpec((TM, D_MODEL), lambda i, j: (i, 0)),
            _gain_spec(g),
            pl.BlockSpec((D_MODEL, SGU_WIDTH), lambda i, j: (0, 1 - j)),
            pl.BlockSpec((1, SGU_WIDTH), fixed2),
            pl.BlockSpec((1, SGU_WIDTH), fixed2),
            pl.BlockSpec((SGU_GROUPS, CHUNK, CHUNK), lambda i, j: (0, 0, 0)),
            pl.BlockSpec((CHUNK, SGU_GROUPS), fixed2),
            pl.BlockSpec((SGU_GROUPS, CHUNK, CHUNK), lambda i, j: (0, 0, 0)),
            pl.BlockSpec((CHUNK, SGU_GROUPS), fixed2),
        ],
        out_specs=[
            pl.BlockSpec((TM, SGU_WIDTH), lambda i, j: (i, 0)),
            pl.BlockSpec((M_SAMPLE, SGU_WIDTH), fixed2),
        ],
        out_shape=[
            jax.ShapeDtypeStruct((M_ALL, SGU_WIDTH), BF16),
            jax.ShapeDtypeStruct((M_SAMPLE, SGU_WIDTH), F32),
        ],
        scratch_shapes=[pltpu.VMEM((TM, D_MODEL), BF16), pltpu.VMEM((TM, SGU_WIDTH), F32)],
        compiler_params=_params("arbitrary", "arbitrary"),
        name="sgu_mixer",
    )(x, g[0], w_in, ln_g, ln_b, ws, bst, ws_s, bst_s)


def _attention_layer(x, g_pre, w_qkv, sinks, cache_k, cache_v, tables):
    q, kv = _attn_proj(x, g_pre, w_qkv, tables)
    o_p, kv_tail = _attn_prompt(q, kv, sinks)

    q_s = q[M_PROMPT:].reshape(DEC_BATCH, DEC_SEQ, Q_TILES, LANES).transpose(0, 2, 1, 3)
    q_s = q_s.reshape(DEC_BATCH, SAMPLE_Q_ROWS, LANES)
    kv_new = kv[M_PROMPT:].reshape(DEC_BATCH, DEC_SEQ, 2 * KV_DIM)
    ck = cache_k.reshape(DEC_BATCH, WINDOW, KV_DIM)
    cv = cache_v.reshape(DEC_BATCH, WINDOW, KV_DIM)
    o_s, new_ks, new_vs = _attn_sample(q_s, ck, cv, kv_new, sinks)
    o_s = o_s.reshape(DEC_BATCH, Q_TILES, DEC_SEQ, LANES).transpose(0, 2, 1, 3).reshape(M_SAMPLE, D_MODEL)

    kv_shape = (BATCH, WINDOW, N_KV_HEADS, HEAD_DIM)
    new_kp = kv_tail[..., :KV_DIM].reshape(kv_shape)
    new_vp = kv_tail[..., KV_DIM:].reshape(kv_shape)
    kv_shape = (DEC_BATCH, WINDOW, N_KV_HEADS, HEAD_DIM)
    return (o_p, o_s), (new_kp, new_vp, new_ks.reshape(kv_shape), new_vs.reshape(kv_shape))


def _conv_layer(x, g_pre, w_in, w_conv, state):
    zeros = jnp.zeros((DEC_BATCH, DEC_SEQ - 1, D_MODEL), F32)
    p1 = jnp.concatenate([state[:, 1:2], zeros], axis=1).reshape(M_SAMPLE, D_MODEL)
    p2 = jnp.concatenate([state[:, 0:1], state[:, 1:2], zeros[:, 1:]], axis=1).reshape(M_SAMPLE, D_MODEL)
    a, z_p, z_s = _conv_mixer(x, g_pre, w_in, w_conv, p1, p2)
    keep = CONV_WIDTH - 1
    new_cp = jnp.stack([z_p[(tile + 1) * SUBLANES - keep:(tile + 1) * SUBLANES]
                        for tile, _ in _prompt_tail_slices()])
    new_cs = z_s.reshape(DEC_BATCH, DEC_SEQ, D_MODEL)[:, DEC_SEQ - keep:]
    return a, (new_cp, new_cs)


def _sgu_layer(x, g_pre, w_in, ln_g, ln_b, w_s, b_s):
    eye = jnp.eye(DEC_BATCH, dtype=F32)
    ws_s = jnp.einsum("ab,gts->gatbs", eye, w_s[:, :DEC_SEQ, :DEC_SEQ]).reshape(SGU_GROUPS, M_SAMPLE, M_SAMPLE)
    bst_s = jnp.tile(b_s[:, :DEC_SEQ].T, (DEC_BATCH, 1))
    a, vln_s = _sgu_mixer(x, g_pre, w_in, ln_g, ln_b, w_s, b_s.T, ws_s, bst_s)
    return a, vln_s.reshape(DEC_BATCH, DEC_SEQ, SGU_WIDTH)


def kernel(x_prompt, x_sample, cache_k, cache_v, state_conv, norm_g, ffn_w_gu, ffn_w_down,
           attn_w_qkv, attn_w_o, attn_sinks, conv_w_in, conv_w, conv_w_out,
           sgu_w_in, sgu_ln_g, sgu_ln_b, sgu_w_s, sgu_b_s, sgu_w_out):
    x = (x_prompt.reshape(M_PROMPT, D_MODEL), x_sample.reshape(M_SAMPLE, D_MODEL))
    tables = _rope_tables()
    gains = norm_g.reshape(-1, 1, D_MODEL)
    gain = lambda i, k: (gains, i * norm_g.shape[1] + k)
    mixer_weights = ((attn_w_qkv, attn_w_o), (conv_w_in, conv_w_out), (sgu_w_in, sgu_w_out))
    w_gu, w_down = ffn_w_gu[0, 0].astype(BF16), ffn_w_down[0, 0].astype(BF16)
    kp_l, vp_l, ks_l, vs_l, cp_l, cs_l, sv_l = [], [], [], [], [], [], []
    for i in range(DEPTH):
        kind, j = i % N_MIXERS, i // N_MIXERS
        w_in_f32, w_out_f32 = mixer_weights[kind]
        x, (w_in, w_out, w_gu, w_down) = _ffn(
            x, gain(i, 0), gain(i, 1), w_gu, w_down,
            casts=((w_in_f32, (j,)), (w_out_f32, (j,)), (ffn_w_gu, (i, 1)), (ffn_w_down, (i, 1))))
        if kind == 0:
            a, (kp, vp, ks, vs) = _attention_layer(x, gain(i, 2), w_in, attn_sinks[j],
                                                   cache_k[j], cache_v[j], tables)
            kp_l.append(kp); vp_l.append(vp); ks_l.append(ks); vs_l.append(vs)
        elif kind == 1:
            a, (cp, cs) = _conv_layer(x, gain(i, 2), w_in, conv_w[j], state_conv[j])
            cp_l.append(cp); cs_l.append(cs)
        else:
            a, sv = _sgu_layer(x, gain(i, 2), w_in, sgu_ln_g[j].reshape(1, SGU_WIDTH),
                               sgu_ln_b[j].reshape(1, SGU_WIDTH), sgu_w_s[j], sgu_b_s[j])
            sv_l.append(sv)
        x = _outproj(x, a, gain(i, 3), w_out)
        last = i + 1 == DEPTH
        casts = () if last else ((ffn_w_gu, (i + 1, 0)), (ffn_w_down, (i + 1, 0)))
        x, next_ffn = _ffn(x, gain(i, 4), gain(i, 5), w_gu, w_down, casts=casts, split_out=last)
        if next_ffn:
            w_gu, w_down = next_ffn
    y_prompt = x[0].reshape(BATCH, SEQ, D_MODEL)
    y_sample = x[1].reshape(DEC_BATCH, DEC_SEQ, D_MODEL)
    return (y_prompt, y_sample, jnp.stack(kp_l), jnp.stack(vp_l), jnp.stack(ks_l), jnp.stack(vs_l),
            jnp.stack(cp_l), jnp.stack(cs_l), jnp.stack(sv_l))
```

```python
import functools

import jax
import jax.numpy as jnp
from jax import lax
from jax.experimental import pallas as pl
from jax.experimental.pallas import tpu as pltpu

D_MODEL = 2048
BATCH = 2
SEQ = 4096
DEPTH = 4
DEC_BATCH = 32
DEC_SEQ = 4
PAST_LEN = 16384
HEAD_DIM = 64
N_HEADS = D_MODEL // HEAD_DIM
N_KV_HEADS = N_HEADS // 8
ROT_DIM = HEAD_DIM // 4
ROPE_THETA = 500000.0
WINDOW = 128
BLOCK = 128
CONV_WIDTH = 3
SGU_WIDTH = D_MODEL
SGU_GROUPS = 8
SGU_GROUP_DIM = SGU_WIDTH // SGU_GROUPS
CHUNK = 128
D_FF = 5632
N_MIXERS = 3
RMS_EPS = 1e-6
LN_EPS = 1e-5
NEG_INF = -1e30
LOG2E = 1.4426950408889634

M_PROMPT = BATCH * SEQ
M_SAMPLE = DEC_BATCH * DEC_SEQ
M_ALL = M_PROMPT + M_SAMPLE
KV_DIM = N_KV_HEADS * HEAD_DIM
QKV_DIM = D_MODEL + 2 * KV_DIM

LANES = 128
SUBLANES = 8
BF16_SUBLANES = 16
MXU_COLS = 256
HEADS_PER_TILE = LANES // HEAD_DIM
Q_TILES = D_MODEL // LANES
TILES_PER_KV = Q_TILES // N_KV_HEADS
KEYS = 2 * BLOCK

TM = 640
N_ROW_TILES = M_ALL // TM
SAMPLE_ROW0 = M_PROMPT - (N_ROW_TILES - 1) * TM
TF = 512
ROW_CHUNK = 16
CONV_TN = 512
VMEM_LIMIT = 58 * 1024 * 1024

assert M_ALL % TM == 0 and SAMPLE_ROW0 + M_SAMPLE == TM and M_SAMPLE == CHUNK and TM % CHUNK == 0
assert M_PROMPT % DEC_SEQ == 0 and SEQ % CHUNK == 0
assert TM % ROW_CHUNK == 0 and SAMPLE_ROW0 % ROW_CHUNK == 0 and ROW_CHUNK % BF16_SUBLANES == 0

F32 = jnp.float32
BF16 = jnp.bfloat16


def _rms(x, g):
    return x * lax.rsqrt(jnp.mean(x * x, axis=-1, keepdims=True) + RMS_EPS) * g


def _params(*sem):
    return pltpu.CompilerParams(dimension_semantics=sem, vmem_limit_bytes=VMEM_LIMIT)


def _gain_spec(gain):
    return pl.BlockSpec((None, 1, D_MODEL), lambda *_: (gain[1], 0, 0))


def _wspec(lead, block, idx):
    return pl.BlockSpec((None,) * len(lead) + block, lambda *g: lead + idx(*g))


def _row_chunks(rows):
    return [slice(c * ROW_CHUNK, (c + 1) * ROW_CHUNK) for c in range(rows // ROW_CHUNK)]


def _x_reader(x_ref, xs_ref, n_tiles, sample_row0):
    def read(rows):
        x = x_ref[rows, :]
        if xs_ref is not None and rows.start >= sample_row0:
            tail = xs_ref[rows.start - sample_row0:rows.stop - sample_row0, :]
            x = jnp.where(pl.program_id(0) == n_tiles - 1, tail, x)
        return x
    return read


def _ffn_kernel(*refs, nj, n_cast, split_in, split_out):
    refs = list(refs)
    x_ref = refs.pop(0)
    xs_ref = refs.pop(0) if split_in else None
    gpre_ref, gpost_ref, wg_ref, wu_ref, wd_ref = refs[:5]
    cast_src = refs[5:5 + n_cast]
    refs = refs[5 + n_cast:]
    o_ref = refs.pop(0)
    os_ref = refs.pop(0) if split_out else None
    cast_dst = refs[:n_cast]
    h_ref, acc_ref = refs[n_cast:]
    i, j = pl.program_id(0), pl.program_id(1)
    read_x = _x_reader(x_ref, xs_ref, N_ROW_TILES, SAMPLE_ROW0)

    @pl.when((i == 0) & (j == 0))
    def _():
        acc_ref[...] = jnp.zeros_like(acc_ref)

    @pl.when(j == 0)
    def _():
        for rows in _row_chunks(TM):
            h_ref[rows, :] = _rms(read_x(rows), gpre_ref[...]).astype(BF16)

    h = h_ref[...]
    g = jnp.dot(h, wg_ref[...], preferred_element_type=F32)
    for src, dst in zip(cast_src, cast_dst):
        dst[...] = src[...].astype(BF16)
    u = jnp.dot(h, wu_ref[...], preferred_element_type=F32)
    a = (g * jax.nn.sigmoid(g) * u).astype(BF16)
    d = jnp.dot(a, wd_ref[...], preferred_element_type=F32)
    acc_ref[...] = jnp.where(j == 0, 0.0, acc_ref[...]) + d

    @pl.when(j == nj - 1)
    def _():
        gain = 0.5 * gpost_ref[...]
        for rows in _row_chunks(TM):
            y = read_x(rows) + _rms(acc_ref[rows, :], gain)
            o_ref[rows, :] = y
            if os_ref is not None and rows.start >= SAMPLE_ROW0:
                os_ref[rows.start - SAMPLE_ROW0:rows.stop - SAMPLE_ROW0, :] = y


def _cast_rows_per_step(rows, steps):
    rps = BF16_SUBLANES
    while rows % rps or rows // rps > steps:
        rps += BF16_SUBLANES
    return rps


def _ffn(x, g_pre, g_post, w_gu, w_down, casts=(), split_out=False):
    ni, nj = N_ROW_TILES, D_FF // TF
    row_tile = pl.BlockSpec((TM, D_MODEL), lambda i, j: (i, 0))
    sample_rows = pl.BlockSpec((M_SAMPLE, D_MODEL), lambda i, j: (0, 0))
    split_in = isinstance(x, tuple)
    x_args, x_specs = (list(x), [row_tile, sample_rows]) if split_in else ([x], [row_tile])
    if split_out:
        o_specs = [row_tile, sample_rows]
        o_shapes = [jax.ShapeDtypeStruct((M_PROMPT, D_MODEL), F32), jax.ShapeDtypeStruct((M_SAMPLE, D_MODEL), F32)]
    else:
        o_specs, o_shapes = [row_tile], [jax.ShapeDtypeStruct((M_ALL, D_MODEL), F32)]
    cast_in_specs, cast_out_specs, cast_shapes, cast_args = [], [], [], []
    for w, lead in casts:
        rows, cols = w.shape[-2:]
        rps = _cast_rows_per_step(rows, ni * nj)
        slab = lambda i, j, last=rows // rps - 1: (jnp.minimum(i * nj + j, last), 0)
        cast_in_specs.append(_wspec(lead, (rps, cols), slab))
        cast_out_specs.append(pl.BlockSpec((rps, cols), slab))
        cast_shapes.append(jax.ShapeDtypeStruct((rows, cols), BF16))
        cast_args.append(w)
    out = pl.pallas_call(
        functools.partial(_ffn_kernel, nj=nj, n_cast=len(casts), split_in=split_in, split_out=split_out),
        grid=(ni, nj),
        in_specs=x_specs + [
            _gain_spec(g_pre),
            _gain_spec(g_post),
            pl.BlockSpec((D_MODEL, TF), lambda i, j: (0, j)),
            pl.BlockSpec((D_MODEL, TF), lambda i, j: (0, j + nj)),
            pl.BlockSpec((TF, D_MODEL), lambda i, j: (j, 0)),
        ] + cast_in_specs,
        out_specs=o_specs + cast_out_specs,
        out_shape=o_shapes + cast_shapes,
        scratch_shapes=[pltpu.VMEM((TM, D_MODEL), BF16), pltpu.VMEM((TM, D_MODEL), F32)],
        compiler_params=_params("arbitrary", "arbitrary"),
        name="ffn",
    )(*x_args, g_pre[0], g_post[0], w_gu, w_gu, w_down, *cast_args)
    n_x = len(o_specs)
    return (tuple(out[:n_x]) if split_out else out[0]), out[n_x:]


def _outproj_kernel(*refs, split_a):
    x_ref, a_ref = refs[:2]
    as_ref = refs[2] if split_a else None
    g_ref, w_ref, o_ref, y_ref = refs[-4:]
    a = a_ref[...]
    if as_ref is not None:
        tail = jnp.where(pl.program_id(0) == N_ROW_TILES - 1, as_ref[...], a_ref[SAMPLE_ROW0:, :])
        a = jnp.concatenate([a_ref[:SAMPLE_ROW0, :], tail], axis=0)
    y_ref[...] = jnp.dot(a, w_ref[...], preferred_element_type=F32)
    for c in range(TM // ROW_CHUNK):
        rows = slice(c * ROW_CHUNK, (c + 1) * ROW_CHUNK)
        o_ref[rows, :] = x_ref[rows, :] + _rms(y_ref[rows, :], g_ref[...])


def _outproj(x, a, g, w):
    row_tile = pl.BlockSpec((TM, D_MODEL), lambda i: (i, 0))
    split_a = isinstance(a, tuple)
    a_args = list(a) if split_a else [a]
    a_specs = [row_tile] + ([pl.BlockSpec((M_SAMPLE, D_MODEL), lambda i: (0, 0))] if split_a else [])
    return pl.pallas_call(
        functools.partial(_outproj_kernel, split_a=split_a),
        grid=(N_ROW_TILES,),
        in_specs=[row_tile] + a_specs + [
            _gain_spec(g),
            pl.BlockSpec((D_MODEL, D_MODEL), lambda i: (0, 0)),
        ],
        out_specs=row_tile,
        out_shape=jax.ShapeDtypeStruct((M_ALL, D_MODEL), F32),
        scratch_shapes=[pltpu.VMEM((TM, D_MODEL), F32)],
        compiler_params=_params("arbitrary"),
        name="outproj",
    )(x, *a_args, g[0], w)


def _rope_tables():
    inv = jnp.power(jnp.float32(ROPE_THETA), -jnp.arange(0, ROT_DIM, 2, dtype=jnp.float32) / ROT_DIM)
    pos = jnp.concatenate([
        jnp.tile(jnp.arange(SEQ), BATCH),
        jnp.tile(PAST_LEN + jnp.arange(DEC_SEQ), DEC_BATCH),
    ])
    ang = pos.astype(jnp.float32)[:, None] * inv[None, :]
    c, s = jnp.cos(ang), jnp.sin(ang)
    pad = jnp.zeros((M_ALL, HEAD_DIM - ROT_DIM), F32)
    zero = jnp.zeros_like(s)
    cos_h = jnp.concatenate([c, c, pad + 1.0], axis=1)
    sa_h = jnp.concatenate([-s, zero, pad], axis=1)
    sb_h = jnp.concatenate([zero, s, pad], axis=1)
    rep = lambda t: jnp.tile(t, (1, HEADS_PER_TILE))
    return rep(cos_h), rep(sa_h), rep(sb_h)


def _attn_proj_kernel(x_ref, g_ref, w_ref, cos_ref, sa_ref, sb_ref, q_ref, kv_ref):
    h = _rms(x_ref[...], g_ref[...]).astype(BF16)
    cos, sa, sb = cos_ref[...], sa_ref[...], sb_ref[...]
    half = ROT_DIM // 2
    for c in range(QKV_DIM // MXU_COLS):
        y2 = jnp.dot(h, w_ref[:, c * MXU_COLS:(c + 1) * MXU_COLS], preferred_element_type=F32)
        for part in range(MXU_COLS // LANES):
            t = c * (MXU_COLS // LANES) + part
            y = y2[:, part * LANES:(part + 1) * LANES]
            if t < Q_TILES + KV_DIM // LANES:
                y = y * cos + pltpu.roll(y, LANES - half, axis=1) * sa + pltpu.roll(y, half, axis=1) * sb
            if t < Q_TILES:
                q_ref[:, t * LANES:(t + 1) * LANES] = (y * (HEAD_DIM ** -0.5 * LOG2E)).astype(BF16)
            else:
                kv_ref[:, (t - Q_TILES) * LANES:(t - Q_TILES + 1) * LANES] = y


def _attn_proj(x, g, w, tables):
    row = lambda i: (i, 0)
    fixed = lambda i: (0, 0)
    return pl.pallas_call(
        _attn_proj_kernel,
        grid=(N_ROW_TILES,),
        in_specs=[
            pl.BlockSpec((TM, D_MODEL), row),
            _gain_spec(g),
            pl.BlockSpec((D_MODEL, QKV_DIM), fixed),
            pl.BlockSpec((TM, LANES), row),
            pl.BlockSpec((TM, LANES), row),
            pl.BlockSpec((TM, LANES), row),
        ],
        out_specs=[pl.BlockSpec((TM, D_MODEL), row), pl.BlockSpec((TM, 2 * KV_DIM), row)],
        out_shape=[jax.ShapeDtypeStruct((M_ALL, D_MODEL), BF16),
                   jax.ShapeDtypeStruct((M_ALL, 2 * KV_DIM), F32)],
        compiler_params=_params("arbitrary"),
        name="attn_proj",
    )(x, g[0], w, *tables)


def _both_halves(tile, first_half):
    lane = lax.broadcasted_iota(jnp.int32, tile.shape, 1)
    rolled = pltpu.roll(tile, HEAD_DIM, axis=1)
    keep = (lane < HEAD_DIM) if first_half else (lane >= HEAD_DIM)
    return jnp.where(keep, tile, rolled)


def _block_diag(rep):
    lane = lax.broadcasted_iota(jnp.int32, rep.shape, 1)
    top = jnp.where(lane < HEAD_DIM, rep, 0.0)
    bot = jnp.where(lane >= HEAD_DIM, rep, 0.0)
    return jnp.concatenate([top, bot], axis=0).astype(BF16)


def _kv_block_diag(k_all, v_all, h):
    lo = (h // HEADS_PER_TILE) * LANES
    first_half = h % HEADS_PER_TILE == 0
    return (_block_diag(_both_halves(k_all[:, lo:lo + LANES], first_half)),
            _block_diag(_both_halves(v_all[:, lo:lo + LANES], first_half)))


def _attend(q, k_bd, v_bd, mask, sinks):
    rows = mask.shape[0]
    s_all = lax.dot_general(q, k_bd, (((1,), (1,)), ((), ())), preferred_element_type=F32)
    lane = lax.broadcasted_iota(jnp.int32, (rows, LANES), 1)
    ps, sink_terms = [], []
    for t, (sink_a, sink_b) in enumerate(sinks):
        s = s_all[t * rows:(t + 1) * rows]
        sa = jnp.where(mask, s[:, :KEYS], NEG_INF)
        sb = jnp.where(mask, s[:, KEYS:], NEG_INF)
        ma = jnp.maximum(jnp.max(sa, axis=-1, keepdims=True), sink_a * LOG2E)
        mb = jnp.maximum(jnp.max(sb, axis=-1, keepdims=True), sink_b * LOG2E)
        ps.append(jnp.concatenate([jnp.exp2(sa - ma), jnp.exp2(sb - mb)], axis=1).astype(BF16))
        sink_terms.append(jnp.where(lane < HEAD_DIM, jnp.exp2(sink_a * LOG2E - ma), jnp.exp2(sink_b * LOG2E - mb)))
    key = lax.broadcasted_iota(jnp.int32, v_bd.shape, 0)
    lane_v = lax.broadcasted_iota(jnp.int32, v_bd.shape, 1)
    ones_bd = jnp.where((key < KEYS) == (lane_v < HEAD_DIM), 1.0, 0.0).astype(BF16)
    o = jnp.dot(jnp.concatenate(ps, axis=0), jnp.concatenate([v_bd, ones_bd], axis=1),
                preferred_element_type=F32)
    return o[:, :LANES] / (o[:, LANES:] + jnp.concatenate(sink_terms, axis=0))


def _window_mask(rows, q_index, min_key):
    kj = lax.broadcasted_iota(jnp.int32, (rows, KEYS), 1)
    return (kj >= q_index) & (kj <= q_index + WINDOW) & (kj >= min_key)


def _attn_prompt_kernel(sink_ref, q_ref, kvc_ref, kvp_ref, o_ref, kv_tail_ref):
    kv_tail_ref[...] = kvc_ref[...]
    min_key = jnp.where(pl.program_id(1) == 0, BLOCK, 0)
    kv = jnp.concatenate([kvp_ref[...], kvc_ref[...]], axis=0)
    qi = lax.broadcasted_iota(jnp.int32, (BLOCK, KEYS), 0)
    mask = _window_mask(BLOCK, qi, min_key)
    for h in range(N_KV_HEADS):
        k_bd, v_bd = _kv_block_diag(kv[:, :KV_DIM], kv[:, KV_DIM:], h)
        tiles = range(h * TILES_PER_KV, (h + 1) * TILES_PER_KV)
        q = jnp.concatenate([q_ref[:, t * LANES:(t + 1) * LANES] for t in tiles], axis=0)
        sinks = [(sink_ref[HEADS_PER_TILE * t], sink_ref[HEADS_PER_TILE * t + 1]) for t in tiles]
        o = _attend(q, k_bd, v_bd, mask, sinks).astype(BF16)
        for n, t in enumerate(tiles):
            o_ref[:, t * LANES:(t + 1) * LANES] = o[n * BLOCK:(n + 1) * BLOCK]


def _attn_prompt(q, kv, sinks):
    nb = SEQ // BLOCK
    return pl.pallas_call(
        _attn_prompt_kernel,
        grid=(BATCH, nb),
        in_specs=[
            pl.BlockSpec(memory_space=pltpu.SMEM),
            pl.BlockSpec((BLOCK, D_MODEL), lambda b, n: (b * nb + n, 0)),
            pl.BlockSpec((BLOCK, 2 * KV_DIM), lambda b, n: (b * nb + n, 0)),
            pl.BlockSpec((BLOCK, 2 * KV_DIM), lambda b, n: (b * nb + jnp.maximum(n - 1, 0), 0)),
        ],
        out_specs=[pl.BlockSpec((BLOCK, D_MODEL), lambda b, n: (b * nb + n, 0)),
                   pl.BlockSpec((None, BLOCK, 2 * KV_DIM), lambda b, n: (b, 0, 0))],
        out_shape=[jax.ShapeDtypeStruct((M_PROMPT, D_MODEL), BF16),
                   jax.ShapeDtypeStruct((BATCH, BLOCK, 2 * KV_DIM), F32)],
        compiler_params=_params("arbitrary", "arbitrary"),
        name="attn_prompt",
    )(sinks, q, kv, kv)


SAMPLE_Q_ROWS = Q_TILES * DEC_SEQ
ROWS_PER_KV = TILES_PER_KV * DEC_SEQ


SAMPLE_BB = 4


def _attn_sample_kernel(sink_ref, q_ref, ck_ref, cv_ref, kvn_ref, o_ref, ok_ref, ov_ref, kbuf, vbuf):
    row = lax.broadcasted_iota(jnp.int32, (ROWS_PER_KV, KEYS), 0)
    mask = _window_mask(ROWS_PER_KV, row % DEC_SEQ, 0)
    tile_of_row = lax.broadcasted_iota(jnp.int32, (ROWS_PER_KV, 1), 0) // DEC_SEQ
    sink_cols = []
    for h in range(N_KV_HEADS):
        sink_a = jnp.zeros((ROWS_PER_KV, 1), F32)
        sink_b = jnp.zeros((ROWS_PER_KV, 1), F32)
        for t in range(TILES_PER_KV):
            tile = h * TILES_PER_KV + t
            sink_a = jnp.where(tile_of_row == t, sink_ref[HEADS_PER_TILE * tile], sink_a)
            sink_b = jnp.where(tile_of_row == t, sink_ref[HEADS_PER_TILE * tile + 1], sink_b)
        sink_cols.append((sink_a, sink_b))
    pad = jnp.zeros((KEYS - WINDOW, KV_DIM), F32)
    for b in range(SAMPLE_BB):
        kbuf[b, WINDOW:, :] = pad
        vbuf[b, WINDOW:, :] = pad
        kbuf[b, :WINDOW, :] = ck_ref[b]
        vbuf[b, :WINDOW, :] = cv_ref[b]
        kbuf[b, WINDOW:WINDOW + DEC_SEQ, :] = kvn_ref[b, :, :KV_DIM]
        vbuf[b, WINDOW:WINDOW + DEC_SEQ, :] = kvn_ref[b, :, KV_DIM:]
        ok_ref[b] = kbuf[b, DEC_SEQ:DEC_SEQ + WINDOW, :]
        ov_ref[b] = vbuf[b, DEC_SEQ:DEC_SEQ + WINDOW, :]
        k_all = kbuf[b]
        v_all = vbuf[b]
        for h in range(N_KV_HEADS):
            k_bd, v_bd = _kv_block_diag(k_all, v_all, h)
            rows = slice(h * ROWS_PER_KV, (h + 1) * ROWS_PER_KV)
            o_ref[b, rows, :] = _attend(q_ref[b, rows, :], k_bd, v_bd, mask, [sink_cols[h]]).astype(BF16)


def _attn_sample(q_s, cache_k, cache_v, kv_new, sinks):
    per_b = lambda b: (b, 0, 0)
    cache_spec = pl.BlockSpec((SAMPLE_BB, WINDOW, KV_DIM), per_b)
    cache_shape = jax.ShapeDtypeStruct((DEC_BATCH, WINDOW, KV_DIM), F32)
    return pl.pallas_call(
        _attn_sample_kernel,
        grid=(DEC_BATCH // SAMPLE_BB,),
        in_specs=[
            pl.BlockSpec(memory_space=pltpu.SMEM),
            pl.BlockSpec((SAMPLE_BB, SAMPLE_Q_ROWS, LANES), per_b),
            cache_spec,
            cache_spec,
            pl.BlockSpec((SAMPLE_BB, DEC_SEQ, 2 * KV_DIM), per_b),
        ],
        out_specs=[pl.BlockSpec((SAMPLE_BB, SAMPLE_Q_ROWS, LANES), per_b), cache_spec, cache_spec],
        out_shape=[jax.ShapeDtypeStruct((DEC_BATCH, SAMPLE_Q_ROWS, LANES), BF16), cache_shape, cache_shape],
        scratch_shapes=[pltpu.VMEM((SAMPLE_BB, KEYS, KV_DIM), F32), pltpu.VMEM((SAMPLE_BB, KEYS, KV_DIM), F32)],
        compiler_params=_params("arbitrary"),
        name="attn_sample",
    )(sinks, q_s, cache_k, cache_v, kv_new)


def _prompt_tail_slices():
    out = []
    for b in range(BATCH):
        end = (b + 1) * SEQ
        tile = (end - 1) // TM
        out.append((tile, end - tile * TM - SUBLANES))
    return out


def _conv_mixer_kernel(x_ref, g_ref, wb_ref, wc_ref, wh_ref, cw_ref, p1_ref, p2_ref,
                       a_ref, zp_ref, zs_ref, h_ref, carry_ref):
    i, j = pl.program_id(0), pl.program_id(1)

    @pl.when(j == 0)
    def _():
        h_ref[...] = _rms(x_ref[...], g_ref[...]).astype(BF16)

    @pl.when(i == 0)
    def _():
        carry_ref[j] = jnp.zeros(carry_ref.shape[1:], F32)

    h = h_ref[...]
    c = jnp.dot(h, wc_ref[...], preferred_element_type=F32)
    hin = jnp.dot(h, wh_ref[...], preferred_element_type=F32)
    z = c * hin
    prev = carry_ref[j]
    carry_ref[j] = z[TM - SUBLANES:]

    row = lax.broadcasted_iota(jnp.int32, (TM, 1), 0)
    grow = row + i * TM
    is_sample = grow >= M_PROMPT
    t = jnp.where(is_sample, grow % DEC_SEQ, grow % SEQ)
    top = jnp.zeros((SAMPLE_ROW0, z.shape[1]), F32)
    pre1 = jnp.where(is_sample, jnp.concatenate([top, p1_ref[...]], axis=0), 0.0)
    pre2 = jnp.where(is_sample, jnp.concatenate([top, p2_ref[...]], axis=0), 0.0)
    last = prev[SUBLANES - 1:SUBLANES]
    zm1 = jnp.where(row == 0, last, pltpu.roll(z, 1, axis=0))
    zm2 = jnp.where(row == 0, prev[SUBLANES - 2:SUBLANES - 1],
                    jnp.where(row == 1, last, pltpu.roll(z, 2, axis=0)))
    zm1 = jnp.where(t < 1, pre1, zm1)
    zm2 = jnp.where(t < 2, pre2, zm2)
    y = cw_ref[0:1, :] * zm2 + cw_ref[1:2, :] * zm1 + cw_ref[2:3, :] * z
    b = jnp.dot(h, wb_ref[...], preferred_element_type=F32)
    a_ref[...] = (b * y).astype(BF16)

    zs_ref[...] = z[SAMPLE_ROW0:]
    tails = _prompt_tail_slices()
    zp = z[tails[0][1]:tails[0][1] + SUBLANES]
    for tile, lo in tails[1:]:
        zp = jnp.where(i == tile, z[lo:lo + SUBLANES], zp)
    zp_ref[...] = zp


def _conv_mixer(x, g, w_in, w_conv, p1, p2):
    nj = D_MODEL // CONV_TN
    col = lambda k: (lambda i, j: (0, j + k * nj))
    return pl.pallas_call(
        _conv_mixer_kernel,
        grid=(N_ROW_TILES, nj),
        in_specs=[
            pl.BlockSpec((TM, D_MODEL), lambda i, j: (i, 0)),
            _gain_spec(g),
            pl.BlockSpec((D_MODEL, CONV_TN), col(0)),
            pl.BlockSpec((D_MODEL, CONV_TN), col(1)),
            pl.BlockSpec((D_MODEL, CONV_TN), col(2)),
            pl.BlockSpec((CONV_WIDTH, CONV_TN), lambda i, j: (0, j)),
            pl.BlockSpec((M_SAMPLE, CONV_TN), lambda i, j: (0, j)),
            pl.BlockSpec((M_SAMPLE, CONV_TN), lambda i, j: (0, j)),
        ],
        out_specs=[
            pl.BlockSpec((TM, CONV_TN), lambda i, j: (i, j)),
            pl.BlockSpec((SUBLANES, CONV_TN), lambda i, j: (i, j)),
            pl.BlockSpec((M_SAMPLE, CONV_TN), lambda i, j: (0, jnp.where(i == N_ROW_TILES - 1, j, 0))),
        ],
        out_shape=[
            jax.ShapeDtypeStruct((M_ALL, D_MODEL), BF16),
            jax.ShapeDtypeStruct((N_ROW_TILES * SUBLANES, D_MODEL), F32),
            jax.ShapeDtypeStruct((M_SAMPLE, D_MODEL), F32),
        ],
        scratch_shapes=[pltpu.VMEM((TM, D_MODEL), BF16), pltpu.VMEM((nj, SUBLANES, CONV_TN), F32)],
        compiler_params=_params("arbitrary", "arbitrary"),
        name="conv_mixer",
    )(x, g[0], w_in, w_in, w_in, w_conv, p1, p2)


def _sgu_mixer_kernel(x_ref, g_ref, w_ref, lg_ref, lb_ref, ws_ref, bst_ref, wss_ref, bsts_ref,
                      a_ref, vln_ref, h_ref, mixed_ref):
    i, j = pl.program_id(0), pl.program_id(1)
    n_chunks = TM // CHUNK

    @pl.when(j == 0)
    def _():
        h = _rms(x_ref[...], g_ref[...]).astype(BF16)
        h_ref[...] = h
        v = jax.nn.gelu(jnp.dot(h, w_ref[...], preferred_element_type=F32))
        mu = jnp.mean(v, axis=-1, keepdims=True)
        xc = v - mu
        vln = xc * lax.rsqrt(jnp.mean(xc * xc, axis=-1, keepdims=True) + LN_EPS) * lg_ref[...] + lb_ref[...]
        vln_ref[...] = vln[SAMPLE_ROW0:]
        vb = vln.astype(BF16)
        r = lax.broadcasted_iota(jnp.int32, (CHUNK, CHUNK), 0)
        c = lax.broadcasted_iota(jnp.int32, (CHUNK, CHUNK), 1)
        causal = r >= c
        last_tile = i == N_ROW_TILES - 1
        for grp in range(SGU_GROUPS):
            cols = slice(grp * SGU_GROUP_DIM, (grp + 1) * SGU_GROUP_DIM)
            w_p = jnp.where(causal, ws_ref[grp], 0.0)
            w_s = jnp.where(last_tile, jnp.where(causal, wss_ref[grp], 0.0), w_p)
            b_p = bst_ref[:, grp:grp + 1]
            b_s = jnp.where(last_tile, bsts_ref[:, grp:grp + 1], b_p)
            for ch in range(n_chunks):
                rows = slice(ch * CHUNK, (ch + 1) * CHUNK)
                sample_chunk = ch == n_chunks - 1
                w = (w_s if sample_chunk else w_p).astype(BF16)
                bias = b_s if sample_chunk else b_p
                mixed_ref[rows, cols] = jnp.dot(w, vb[rows, cols], preferred_element_type=F32) + bias

    @pl.when(j == 1)
    def _():
        u = jax.nn.gelu(jnp.dot(h_ref[...], w_ref[...], preferred_element_type=F32))
        a_ref[...] = (u * mixed_ref[...]).astype(BF16)


def _sgu_mixer(x, g, w_in, ln_g, ln_b, ws, bst, ws_s, bst_s):
    fixed2 = lambda i, j: (0, 0)
    return pl.pallas_call(
        _sgu_mixer_kernel,
        grid=(N_ROW_TILES, 2),
        in_specs=[
            pl.BlockSpec((TM, D_MODEL), lambda i, j: (i, 0)),
            _gain_spec(g),
            pl.BlockSpec((D_MODEL, SGU_WIDTH), lambda i, j: (0, 1 - j)),
            pl.BlockSpec((1, SGU_WIDTH), fixed2),
            pl.BlockSpec((1, SGU_WIDTH), fixed2),
            pl.BlockSpec((SGU_GROUPS, CHUNK, CHUNK), lambda i, j: (0, 0, 0)),
            pl.BlockSpec((CHUNK, SGU_GROUPS), fixed2),
            pl.BlockSpec((SGU_GROUPS, CHUNK, CHUNK), lambda i, j: (0, 0, 0)),
            pl.BlockSpec((CHUNK, SGU_GROUPS), fixed2),
        ],
        out_specs=[
            pl.BlockSpec((TM, SGU_WIDTH), lambda i, j: (i, 0)),
            pl.BlockSpec((M_SAMPLE, SGU_WIDTH), fixed2),
        ],
        out_shape=[
            jax.ShapeDtypeStruct((M_ALL, SGU_WIDTH), BF16),
            jax.ShapeDtypeStruct((M_SAMPLE, SGU_WIDTH), F32),
        ],
        scratch_shapes=[pltpu.VMEM((TM, D_MODEL), BF16), pltpu.VMEM((TM, SGU_WIDTH), F32)],
        compiler_params=_params("arbitrary", "arbitrary"),
        name="sgu_mixer",
    )(x, g[0], w_in, ln_g, ln_b, ws, bst, ws_s, bst_s)


def _attention_layer(x, g_pre, w_qkv, sinks, cache_k, cache_v, tables):
    q, kv = _attn_proj(x, g_pre, w_qkv, tables)
    o_p, kv_tail = _attn_prompt(q, kv, sinks)

    q_s = q[M_PROMPT:].reshape(DEC_BATCH, DEC_SEQ, Q_TILES, LANES).transpose(0, 2, 1, 3)
    q_s = q_s.reshape(DEC_BATCH, SAMPLE_Q_ROWS, LANES)
    kv_new = kv[M_PROMPT:].reshape(DEC_BATCH, DEC_SEQ, 2 * KV_DIM)
    ck = cache_k.reshape(DEC_BATCH, WINDOW, KV_DIM)
    cv = cache_v.reshape(DEC_BATCH, WINDOW, KV_DIM)
    o_s, new_ks, new_vs = _attn_sample(q_s, ck, cv, kv_new, sinks)
    o_s = o_s.reshape(DEC_BATCH, Q_TILES, DEC_SEQ, LANES).transpose(0, 2, 1, 3).reshape(M_SAMPLE, D_MODEL)

    kv_shape = (BATCH, WINDOW, N_KV_HEADS, HEAD_DIM)
    new_kp = kv_tail[..., :KV_DIM].reshape(kv_shape)
    new_vp = kv_tail[..., KV_DIM:].reshape(kv_shape)
    kv_shape = (DEC_BATCH, WINDOW, N_KV_HEADS, HEAD_DIM)
    return (o_p, o_s), (new_kp, new_vp, new_ks.reshape(kv_shape), new_vs.reshape(kv_shape))


def _conv_layer(x, g_pre, w_in, w_conv, state):
    zeros = jnp.zeros((DEC_BATCH, DEC_SEQ - 1, D_MODEL), F32)
    p1 = jnp.concatenate([state[:, 1:2], zeros], axis=1).reshape(M_SAMPLE, D_MODEL)
    p2 = jnp.concatenate([state[:, 0:1], state[:, 1:2], zeros[:, 1:]], axis=1).reshape(M_SAMPLE, D_MODEL)
    a, z_p, z_s = _conv_mixer(x, g_pre, w_in, w_conv, p1, p2)
    keep = CONV_WIDTH - 1
    new_cp = jnp.stack([z_p[(tile + 1) * SUBLANES - keep:(tile + 1) * SUBLANES]
                        for tile, _ in _prompt_tail_slices()])
    new_cs = z_s.reshape(DEC_BATCH, DEC_SEQ, D_MODEL)[:, DEC_SEQ - keep:]
    return a, (new_cp, new_cs)


def _sgu_layer(x, g_pre, w_in, ln_g, ln_b, w_s, b_s):
    eye = jnp.eye(DEC_BATCH, dtype=F32)
    ws_s = jnp.einsum("ab,gts->gatbs", eye, w_s[:, :DEC_SEQ, :DEC_SEQ]).reshape(SGU_GROUPS, M_SAMPLE, M_SAMPLE)
    bst_s = jnp.tile(b_s[:, :DEC_SEQ].T, (DEC_BATCH, 1))
    a, vln_s = _sgu_mixer(x, g_pre, w_in, ln_g, ln_b, w_s, b_s.T, ws_s, bst_s)
    return a, vln_s.reshape(DEC_BATCH, DEC_SEQ, SGU_WIDTH)


def kernel(x_prompt, x_sample, cache_k, cache_v, state_conv, norm_g, ffn_w_gu, ffn_w_down,
           attn_w_qkv, attn_w_o, attn_sinks, conv_w_in, conv_w, conv_w_out,
           sgu_w_in, sgu_ln_g, sgu_ln_b, sgu_w_s, sgu_b_s, sgu_w_out):
    x = (x_prompt.reshape(M_PROMPT, D_MODEL), x_sample.reshape(M_SAMPLE, D_MODEL))
    tables = _rope_tables()
    gains = norm_g.reshape(-1, 1, D_MODEL)
    gain = lambda i, k: (gains, i * norm_g.shape[1] + k)
    mixer_weights = ((attn_w_qkv, attn_w_o), (conv_w_in, conv_w_out), (sgu_w_in, sgu_w_out))
    w_gu, w_down = ffn_w_gu[0, 0].astype(BF16), ffn_w_down[0, 0].astype(BF16)
    kp_l, vp_l, ks_l, vs_l, cp_l, cs_l, sv_l = [], [], [], [], [], [], []
    for i in range(DEPTH):
        kind, j = i % N_MIXERS, i // N_MIXERS
        w_in_f32, w_out_f32 = mixer_weights[kind]
        x, (w_in, w_out, w_gu, w_down) = _ffn(
            x, gain(i, 0), gain(i, 1), w_gu, w_down,
            casts=((w_in_f32, (j,)), (w_out_f32, (j,)), (ffn_w_gu, (i, 1)), (ffn_w_down, (i, 1))))
        if kind == 0:
            a, (kp, vp, ks, vs) = _attention_layer(x, gain(i, 2), w_in, attn_sinks[j],
                                                   cache_k[j], cache_v[j], tables)
            kp_l.append(kp); vp_l.append(vp); ks_l.append(ks); vs_l.append(vs)
        elif kind == 1:
            a, (cp, cs) = _conv_layer(x, gain(i, 2), w_in, conv_w[j], state_conv[j])
            cp_l.append(cp); cs_l.append(cs)
        else:
            a, sv = _sgu_layer(x, gain(i, 2), w_in, sgu_ln_g[j].reshape(1, SGU_WIDTH),
                               sgu_ln_b[j].reshape(1, SGU_WIDTH), sgu_w_s[j], sgu_b_s[j])
            sv_l.append(sv)
        x = _outproj(x, a, gain(i, 3), w_out)
        last = i + 1 == DEPTH
        casts = () if last else ((ffn_w_gu, (i + 1, 0)), (ffn_w_down, (i + 1, 0)))
        x, next_ffn = _ffn(x, gain(i, 4), gain(i, 5), w_gu, w_down, casts=casts, split_out=last)
        if next_ffn:
            w_gu, w_down = next_ffn
    y_prompt = x[0].reshape(BATCH, SEQ, D_MODEL)
    y_sample = x[1].reshape(DEC_BATCH, DEC_SEQ, D_MODEL)
    return (y_prompt, y_sample, jnp.stack(kp_l), jnp.stack(vp_l), jnp.stack(ks_l), jnp.stack(vs_l),
            jnp.stack(cp_l), jnp.stack(cs_l), jnp.stack(sv_l))
```

```python
import functools

import jax
import jax.numpy as jnp
from jax import lax
from jax.experimental import pallas as pl
from jax.experimental.pallas import tpu as pltpu

D_MODEL = 2048
BATCH = 2
SEQ = 4096
DEPTH = 4
DEC_BATCH = 32
DEC_SEQ = 4
PAST_LEN = 16384
HEAD_DIM = 64
N_HEADS = D_MODEL // HEAD_DIM
N_KV_HEADS = N_HEADS // 8
ROT_DIM = HEAD_DIM // 4
ROPE_THETA = 500000.0
WINDOW = 128
BLOCK = 128
CONV_WIDTH = 3
SGU_WIDTH = D_MODEL
SGU_GROUPS = 8
SGU_GROUP_DIM = SGU_WIDTH // SGU_GROUPS
CHUNK = 128
D_FF = 5632
N_MIXERS = 3
RMS_EPS = 1e-6
LN_EPS = 1e-5
NEG_INF = -1e30
LOG2E = 1.4426950408889634

M_PROMPT = BATCH * SEQ
M_SAMPLE = DEC_BATCH * DEC_SEQ
M_ALL = M_PROMPT + M_SAMPLE
KV_DIM = N_KV_HEADS * HEAD_DIM
QKV_DIM = D_MODEL + 2 * KV_DIM

LANES = 128
SUBLANES = 8
BF16_SUBLANES = 16
MXU_COLS = 256
HEADS_PER_TILE = LANES // HEAD_DIM
Q_TILES = D_MODEL // LANES
TILES_PER_KV = Q_TILES // N_KV_HEADS
KEYS = 2 * BLOCK

TM = 640
N_ROW_TILES = M_ALL // TM
SAMPLE_ROW0 = M_PROMPT - (N_ROW_TILES - 1) * TM
TF = 512
ROW_CHUNK = 16
CONV_TN = 512
PROMPT_BLOCKS = 4
VMEM_LIMIT = 58 * 1024 * 1024

assert M_ALL % TM == 0 and SAMPLE_ROW0 + M_SAMPLE == TM and M_SAMPLE == CHUNK and TM % CHUNK == 0
assert M_PROMPT % DEC_SEQ == 0 and SEQ % CHUNK == 0
assert TM % ROW_CHUNK == 0 and SAMPLE_ROW0 % ROW_CHUNK == 0 and ROW_CHUNK % BF16_SUBLANES == 0

F32 = jnp.float32
BF16 = jnp.bfloat16


def _rms(x, g):
    return x * lax.rsqrt(jnp.mean(x * x, axis=-1, keepdims=True) + RMS_EPS) * g


def _params(*sem):
    return pltpu.CompilerParams(dimension_semantics=sem, vmem_limit_bytes=VMEM_LIMIT)


def _gain_spec(gain):
    return pl.BlockSpec((None, 1, D_MODEL), lambda *_: (gain[1], 0, 0))


def _wspec(lead, block, idx):
    return pl.BlockSpec((None,) * len(lead) + block, lambda *g: lead + idx(*g))


def _row_chunks(rows):
    return [slice(c * ROW_CHUNK, (c + 1) * ROW_CHUNK) for c in range(rows // ROW_CHUNK)]


def _x_reader(x_ref, xs_ref, n_tiles, sample_row0):
    def read(rows):
        x = x_ref[rows, :]
        if xs_ref is not None and rows.start >= sample_row0:
            tail = xs_ref[rows.start - sample_row0:rows.stop - sample_row0, :]
            x = jnp.where(pl.program_id(0) == n_tiles - 1, tail, x)
        return x
    return read


def _ffn_kernel(*refs, nj, n_cast, split_in, split_out):
    refs = list(refs)
    x_ref = refs.pop(0)
    xs_ref = refs.pop(0) if split_in else None
    gpre_ref, gpost_ref, wg_ref, wu_ref, wd_ref = refs[:5]
    cast_src = refs[5:5 + n_cast]
    refs = refs[5 + n_cast:]
    o_ref = refs.pop(0)
    os_ref = refs.pop(0) if split_out else None
    cast_dst = refs[:n_cast]
    h_ref, acc_ref = refs[n_cast:]
    i, j = pl.program_id(0), pl.program_id(1)
    read_x = _x_reader(x_ref, xs_ref, N_ROW_TILES, SAMPLE_ROW0)

    @pl.when((i == 0) & (j == 0))
    def _():
        acc_ref[...] = jnp.zeros_like(acc_ref)

    @pl.when(j == 0)
    def _():
        for rows in _row_chunks(TM):
            h_ref[rows, :] = _rms(read_x(rows), gpre_ref[...]).astype(BF16)

    h = h_ref[...]
    g = jnp.dot(h, wg_ref[...], preferred_element_type=F32)
    for src, dst in zip(cast_src, cast_dst):
        dst[...] = src[...].astype(BF16)
    u = jnp.dot(h, wu_ref[...], preferred_element_type=F32)
    a = (g * jax.nn.sigmoid(g) * u).astype(BF16)
    d = jnp.dot(a, wd_ref[...], preferred_element_type=F32)
    acc_ref[...] = jnp.where(j == 0, 0.0, acc_ref[...]) + d

    @pl.when(j == nj - 1)
    def _():
        gain = 0.5 * gpost_ref[...]
        for rows in _row_chunks(TM):
            y = read_x(rows) + _rms(acc_ref[rows, :], gain)
            o_ref[rows, :] = y
            if os_ref is not None and rows.start >= SAMPLE_ROW0:
                os_ref[rows.start - SAMPLE_ROW0:rows.stop - SAMPLE_ROW0, :] = y


def _cast_rows_per_step(rows, steps):
    rps = BF16_SUBLANES
    while rows % rps or rows // rps > steps:
        rps += BF16_SUBLANES
    return rps


def _ffn(x, g_pre, g_post, w_gu, w_down, casts=(), split_out=False):
    ni, nj = N_ROW_TILES, D_FF // TF
    row_tile = pl.BlockSpec((TM, D_MODEL), lambda i, j: (i, 0))
    sample_rows = pl.BlockSpec((M_SAMPLE, D_MODEL), lambda i, j: (0, 0))
    split_in = isinstance(x, tuple)
    x_args, x_specs = (list(x), [row_tile, sample_rows]) if split_in else ([x], [row_tile])
    if split_out:
        o_specs = [row_tile, sample_rows]
        o_shapes = [jax.ShapeDtypeStruct((M_PROMPT, D_MODEL), F32), jax.ShapeDtypeStruct((M_SAMPLE, D_MODEL), F32)]
    else:
        o_specs, o_shapes = [row_tile], [jax.ShapeDtypeStruct((M_ALL, D_MODEL), F32)]
    cast_in_specs, cast_out_specs, cast_shapes, cast_args = [], [], [], []
    for w, lead in casts:
        rows, cols = w.shape[-2:]
        rps = _cast_rows_per_step(rows, ni * nj)
        slab = lambda i, j, last=rows // rps - 1: (jnp.minimum(i * nj + j, last), 0)
        cast_in_specs.append(_wspec(lead, (rps, cols), slab))
        cast_out_specs.append(pl.BlockSpec((rps, cols), slab))
        cast_shapes.append(jax.ShapeDtypeStruct((rows, cols), BF16))
        cast_args.append(w)
    out = pl.pallas_call(
        functools.partial(_ffn_kernel, nj=nj, n_cast=len(casts), split_in=split_in, split_out=split_out),
        grid=(ni, nj),
        in_specs=x_specs + [
            _gain_spec(g_pre),
            _gain_spec(g_post),
            pl.BlockSpec((D_MODEL, TF), lambda i, j: (0, j)),
            pl.BlockSpec((D_MODEL, TF), lambda i, j: (0, j + nj)),
            pl.BlockSpec((TF, D_MODEL), lambda i, j: (j, 0)),
        ] + cast_in_specs,
        out_specs=o_specs + cast_out_specs,
        out_shape=o_shapes + cast_shapes,
        scratch_shapes=[pltpu.VMEM((TM, D_MODEL), BF16), pltpu.VMEM((TM, D_MODEL), F32)],
        compiler_params=_params("arbitrary", "arbitrary"),
        name="ffn",
    )(*x_args, g_pre[0], g_post[0], w_gu, w_gu, w_down, *cast_args)
    n_x = len(o_specs)
    return (tuple(out[:n_x]) if split_out else out[0]), out[n_x:]


def _outproj_kernel(*refs, split_a):
    x_ref, a_ref = refs[:2]
    as_ref = refs[2] if split_a else None
    g_ref, w_ref, o_ref, y_ref = refs[-4:]
    a = a_ref[...]
    if as_ref is not None:
        tail = jnp.where(pl.program_id(0) == N_ROW_TILES - 1, as_ref[...], a_ref[SAMPLE_ROW0:, :])
        a = jnp.concatenate([a_ref[:SAMPLE_ROW0, :], tail], axis=0)
    y_ref[...] = jnp.dot(a, w_ref[...], preferred_element_type=F32)
    for c in range(TM // ROW_CHUNK):
        rows = slice(c * ROW_CHUNK, (c + 1) * ROW_CHUNK)
        o_ref[rows, :] = x_ref[rows, :] + _rms(y_ref[rows, :], g_ref[...])


def _outproj(x, a, g, w):
    row_tile = pl.BlockSpec((TM, D_MODEL), lambda i: (i, 0))
    split_a = isinstance(a, tuple)
    a_args = list(a) if split_a else [a]
    a_specs = [row_tile] + ([pl.BlockSpec((M_SAMPLE, D_MODEL), lambda i: (0, 0))] if split_a else [])
    return pl.pallas_call(
        functools.partial(_outproj_kernel, split_a=split_a),
        grid=(N_ROW_TILES,),
        in_specs=[row_tile] + a_specs + [
            _gain_spec(g),
            pl.BlockSpec((D_MODEL, D_MODEL), lambda i: (0, 0)),
        ],
        out_specs=row_tile,
        out_shape=jax.ShapeDtypeStruct((M_ALL, D_MODEL), F32),
        scratch_shapes=[pltpu.VMEM((TM, D_MODEL), F32)],
        compiler_params=_params("arbitrary"),
        name="outproj",
    )(x, *a_args, g[0], w)


def _rope_tables():
    inv = jnp.power(jnp.float32(ROPE_THETA), -jnp.arange(0, ROT_DIM, 2, dtype=jnp.float32) / ROT_DIM)
    pos = jnp.concatenate([
        jnp.tile(jnp.arange(SEQ), BATCH),
        jnp.tile(PAST_LEN + jnp.arange(DEC_SEQ), DEC_BATCH),
    ])
    ang = pos.astype(jnp.float32)[:, None] * inv[None, :]
    c, s = jnp.cos(ang), jnp.sin(ang)
    pad = jnp.zeros((M_ALL, HEAD_DIM - ROT_DIM), F32)
    zero = jnp.zeros_like(s)
    cos_h = jnp.concatenate([c, c, pad + 1.0], axis=1)
    sa_h = jnp.concatenate([-s, zero, pad], axis=1)
    sb_h = jnp.concatenate([zero, s, pad], axis=1)
    rep = lambda t: jnp.tile(t, (1, HEADS_PER_TILE))
    return rep(cos_h), rep(sa_h), rep(sb_h)


def _attn_proj_kernel(x_ref, g_ref, w_ref, cos_ref, sa_ref, sb_ref, q_ref, kv_ref):
    h = _rms(x_ref[...], g_ref[...]).astype(BF16)
    cos, sa, sb = cos_ref[...], sa_ref[...], sb_ref[...]
    half = ROT_DIM // 2
    for c in range(QKV_DIM // MXU_COLS):
        y2 = jnp.dot(h, w_ref[:, c * MXU_COLS:(c + 1) * MXU_COLS], preferred_element_type=F32)
        for part in range(MXU_COLS // LANES):
            t = c * (MXU_COLS // LANES) + part
            y = y2[:, part * LANES:(part + 1) * LANES]
            if t < Q_TILES + KV_DIM // LANES:
                y = y * cos + pltpu.roll(y, LANES - half, axis=1) * sa + pltpu.roll(y, half, axis=1) * sb
            if t < Q_TILES:
                q_ref[:, t * LANES:(t + 1) * LANES] = (y * (HEAD_DIM ** -0.5 * LOG2E)).astype(BF16)
            else:
                kv_ref[:, (t - Q_TILES) * LANES:(t - Q_TILES + 1) * LANES] = y


def _attn_proj(x, g, w, tables):
    row = lambda i: (i, 0)
    fixed = lambda i: (0, 0)
    return pl.pallas_call(
        _attn_proj_kernel,
        grid=(N_ROW_TILES,),
        in_specs=[
            pl.BlockSpec((TM, D_MODEL), row),
            _gain_spec(g),
            pl.BlockSpec((D_MODEL, QKV_DIM), fixed),
            pl.BlockSpec((TM, LANES), row),
            pl.BlockSpec((TM, LANES), row),
            pl.BlockSpec((TM, LANES), row),
        ],
        out_specs=[pl.BlockSpec((TM, D_MODEL), row), pl.BlockSpec((TM, 2 * KV_DIM), row)],
        out_shape=[jax.ShapeDtypeStruct((M_ALL, D_MODEL), BF16),
                   jax.ShapeDtypeStruct((M_ALL, 2 * KV_DIM), F32)],
        compiler_params=_params("arbitrary"),
        name="attn_proj",
    )(x, g[0], w, *tables)


def _both_halves(tile, first_half):
    lane = lax.broadcasted_iota(jnp.int32, tile.shape, 1)
    rolled = pltpu.roll(tile, HEAD_DIM, axis=1)
    keep = (lane < HEAD_DIM) if first_half else (lane >= HEAD_DIM)
    return jnp.where(keep, tile, rolled)


def _block_diag(rep):
    lane = lax.broadcasted_iota(jnp.int32, rep.shape, 1)
    top = jnp.where(lane < HEAD_DIM, rep, 0.0)
    bot = jnp.where(lane >= HEAD_DIM, rep, 0.0)
    return jnp.concatenate([top, bot], axis=0).astype(BF16)


def _kv_block_diag(k_all, v_all, h):
    lo = (h // HEADS_PER_TILE) * LANES
    first_half = h % HEADS_PER_TILE == 0
    return (_block_diag(_both_halves(k_all[:, lo:lo + LANES], first_half)),
            _block_diag(_both_halves(v_all[:, lo:lo + LANES], first_half)))


def _attend(q, k_bd, v_bd, mask, sinks):
    rows = mask.shape[0]
    s_all = lax.dot_general(q, k_bd, (((1,), (1,)), ((), ())), preferred_element_type=F32)
    lane = lax.broadcasted_iota(jnp.int32, (rows, LANES), 1)
    ps, sink_terms = [], []
    for t, (sink_a, sink_b) in enumerate(sinks):
        s = s_all[t * rows:(t + 1) * rows]
        sa = jnp.where(mask, s[:, :KEYS], NEG_INF)
        sb = jnp.where(mask, s[:, KEYS:], NEG_INF)
        ma = jnp.maximum(jnp.max(sa, axis=-1, keepdims=True), sink_a * LOG2E)
        mb = jnp.maximum(jnp.max(sb, axis=-1, keepdims=True), sink_b * LOG2E)
        ps.append(jnp.concatenate([jnp.exp2(sa - ma), jnp.exp2(sb - mb)], axis=1).astype(BF16))
        sink_terms.append(jnp.where(lane < HEAD_DIM, jnp.exp2(sink_a * LOG2E - ma), jnp.exp2(sink_b * LOG2E - mb)))
    key = lax.broadcasted_iota(jnp.int32, v_bd.shape, 0)
    lane_v = lax.broadcasted_iota(jnp.int32, v_bd.shape, 1)
    ones_bd = jnp.where((key < KEYS) == (lane_v < HEAD_DIM), 1.0, 0.0).astype(BF16)
    o = jnp.dot(jnp.concatenate(ps, axis=0), jnp.concatenate([v_bd, ones_bd], axis=1),
                preferred_element_type=F32)
    return o[:, :LANES] / (o[:, LANES:] + jnp.concatenate(sink_terms, axis=0))


def _window_mask(rows, q_index, min_key):
    kj = lax.broadcasted_iota(jnp.int32, (rows, KEYS), 1)
    return (kj >= q_index) & (kj <= q_index + WINDOW) & (kj >= min_key)


def _attn_prompt_kernel(sink_ref, q_ref, kvc_ref, kvp_ref, o_ref, kv_tail_ref):
    kv_tail_ref[...] = kvc_ref[(PROMPT_BLOCKS - 1) * BLOCK:, :]
    qi = lax.broadcasted_iota(jnp.int32, (BLOCK, KEYS), 0)
    for blk in range(PROMPT_BLOCKS):
        rows = slice(blk * BLOCK, (blk + 1) * BLOCK)
        if blk == 0:
            prev, min_key = kvp_ref[...], jnp.where(pl.program_id(1) == 0, BLOCK, 0)
        else:
            prev, min_key = kvc_ref[(blk - 1) * BLOCK:blk * BLOCK, :], 0
        kv = jnp.concatenate([prev, kvc_ref[rows, :]], axis=0)
        mask = _window_mask(BLOCK, qi, min_key)
        for h in range(N_KV_HEADS):
            k_bd, v_bd = _kv_block_diag(kv[:, :KV_DIM], kv[:, KV_DIM:], h)
            tiles = range(h * TILES_PER_KV, (h + 1) * TILES_PER_KV)
            q = jnp.concatenate([q_ref[rows, t * LANES:(t + 1) * LANES] for t in tiles], axis=0)
            sinks = [(sink_ref[HEADS_PER_TILE * t], sink_ref[HEADS_PER_TILE * t + 1]) for t in tiles]
            o = _attend(q, k_bd, v_bd, mask, sinks).astype(BF16)
            for n, t in enumerate(tiles):
                o_ref[rows, t * LANES:(t + 1) * LANES] = o[n * BLOCK:(n + 1) * BLOCK]


def _attn_prompt(q, kv, sinks):
    step_rows = PROMPT_BLOCKS * BLOCK
    nb = SEQ // step_rows
    prev_block = lambda b, n: ((b * nb + n) * PROMPT_BLOCKS - jnp.where(n == 0, 0, 1), 0)
    return pl.pallas_call(
        _attn_prompt_kernel,
        grid=(BATCH, nb),
        in_specs=[
            pl.BlockSpec(memory_space=pltpu.SMEM),
            pl.BlockSpec((step_rows, D_MODEL), lambda b, n: (b * nb + n, 0)),
            pl.BlockSpec((step_rows, 2 * KV_DIM), lambda b, n: (b * nb + n, 0)),
            pl.BlockSpec((BLOCK, 2 * KV_DIM), prev_block),
        ],
        out_specs=[pl.BlockSpec((step_rows, D_MODEL), lambda b, n: (b * nb + n, 0)),
                   pl.BlockSpec((None, BLOCK, 2 * KV_DIM), lambda b, n: (b, 0, 0))],
        out_shape=[jax.ShapeDtypeStruct((M_PROMPT, D_MODEL), BF16),
                   jax.ShapeDtypeStruct((BATCH, BLOCK, 2 * KV_DIM), F32)],
        compiler_params=_params("arbitrary", "arbitrary"),
        name="attn_prompt",
    )(sinks, q, kv, kv)


SAMPLE_Q_ROWS = Q_TILES * DEC_SEQ
ROWS_PER_KV = TILES_PER_KV * DEC_SEQ


SAMPLE_BB = 4


def _attn_sample_kernel(sink_ref, q_ref, ck_ref, cv_ref, kvn_ref, o_ref, ok_ref, ov_ref, kbuf, vbuf):
    row = lax.broadcasted_iota(jnp.int32, (ROWS_PER_KV, KEYS), 0)
    mask = _window_mask(ROWS_PER_KV, row % DEC_SEQ, 0)
    tile_of_row = lax.broadcasted_iota(jnp.int32, (ROWS_PER_KV, 1), 0) // DEC_SEQ
    sink_cols = []
    for h in range(N_KV_HEADS):
        sink_a = jnp.zeros((ROWS_PER_KV, 1), F32)
        sink_b = jnp.zeros((ROWS_PER_KV, 1), F32)
        for t in range(TILES_PER_KV):
            tile = h * TILES_PER_KV + t
            sink_a = jnp.where(tile_of_row == t, sink_ref[HEADS_PER_TILE * tile], sink_a)
            sink_b = jnp.where(tile_of_row == t, sink_ref[HEADS_PER_TILE * tile + 1], sink_b)
        sink_cols.append((sink_a, sink_b))
    pad = jnp.zeros((KEYS - WINDOW, KV_DIM), F32)
    for b in range(SAMPLE_BB):
        kbuf[b, WINDOW:, :] = pad
        vbuf[b, WINDOW:, :] = pad
        kbuf[b, :WINDOW, :] = ck_ref[b]
        vbuf[b, :WINDOW, :] = cv_ref[b]
        kbuf[b, WINDOW:WINDOW + DEC_SEQ, :] = kvn_ref[b, :, :KV_DIM]
        vbuf[b, WINDOW:WINDOW + DEC_SEQ, :] = kvn_ref[b, :, KV_DIM:]
        ok_ref[b] = kbuf[b, DEC_SEQ:DEC_SEQ + WINDOW, :]
        ov_ref[b] = vbuf[b, DEC_SEQ:DEC_SEQ + WINDOW, :]
        k_all = kbuf[b]
        v_all = vbuf[b]
        for h in range(N_KV_HEADS):
            k_bd, v_bd = _kv_block_diag(k_all, v_all, h)
            rows = slice(h * ROWS_PER_KV, (h + 1) * ROWS_PER_KV)
            o_ref[b, rows, :] = _attend(q_ref[b, rows, :], k_bd, v_bd, mask, [sink_cols[h]]).astype(BF16)


def _attn_sample(q_s, cache_k, cache_v, kv_new, sinks):
    per_b = lambda b: (b, 0, 0)
    cache_spec = pl.BlockSpec((SAMPLE_BB, WINDOW, KV_DIM), per_b)
    cache_shape = jax.ShapeDtypeStruct((DEC_BATCH, WINDOW, KV_DIM), F32)
    return pl.pallas_call(
        _attn_sample_kernel,
        grid=(DEC_BATCH // SAMPLE_BB,),
        in_specs=[
            pl.BlockSpec(memory_space=pltpu.SMEM),
            pl.BlockSpec((SAMPLE_BB, SAMPLE_Q_ROWS, LANES), per_b),
            cache_spec,
            cache_spec,
            pl.BlockSpec((SAMPLE_BB, DEC_SEQ, 2 * KV_DIM), per_b),
        ],
        out_specs=[pl.BlockSpec((SAMPLE_BB, SAMPLE_Q_ROWS, LANES), per_b), cache_spec, cache_spec],
        out_shape=[jax.ShapeDtypeStruct((DEC_BATCH, SAMPLE_Q_ROWS, LANES), BF16), cache_shape, cache_shape],
        scratch_shapes=[pltpu.VMEM((SAMPLE_BB, KEYS, KV_DIM), F32), pltpu.VMEM((SAMPLE_BB, KEYS, KV_DIM), F32)],
        compiler_params=_params("arbitrary"),
        name="attn_sample",
    )(sinks, q_s, cache_k, cache_v, kv_new)


def _prompt_tail_slices():
    out = []
    for b in range(BATCH):
        end = (b + 1) * SEQ
        tile = (end - 1) // TM
        out.append((tile, end - tile * TM - SUBLANES))
    return out


def _conv_mixer_kernel(x_ref, g_ref, wb_ref, wc_ref, wh_ref, cw_ref, p1_ref, p2_ref,
                       a_ref, zp_ref, zs_ref, h_ref, carry_ref):
    i, j = pl.program_id(0), pl.program_id(1)

    @pl.when(j == 0)
    def _():
        h_ref[...] = _rms(x_ref[...], g_ref[...]).astype(BF16)

    @pl.when(i == 0)
    def _():
        carry_ref[j] = jnp.zeros(carry_ref.shape[1:], F32)

    h = h_ref[...]
    c = jnp.dot(h, wc_ref[...], preferred_element_type=F32)
    hin = jnp.dot(h, wh_ref[...], preferred_element_type=F32)
    z = c * hin
    prev = carry_ref[j]
    carry_ref[j] = z[TM - SUBLANES:]

    row = lax.broadcasted_iota(jnp.int32, (TM, 1), 0)
    grow = row + i * TM
    is_sample = grow >= M_PROMPT
    t = jnp.where(is_sample, grow % DEC_SEQ, grow % SEQ)
    top = jnp.zeros((SAMPLE_ROW0, z.shape[1]), F32)
    pre1 = jnp.where(is_sample, jnp.concatenate([top, p1_ref[...]], axis=0), 0.0)
    pre2 = jnp.where(is_sample, jnp.concatenate([top, p2_ref[...]], axis=0), 0.0)
    last = prev[SUBLANES - 1:SUBLANES]
    zm1 = jnp.where(row == 0, last, pltpu.roll(z, 1, axis=0))
    zm2 = jnp.where(row == 0, prev[SUBLANES - 2:SUBLANES - 1],
                    jnp.where(row == 1, last, pltpu.roll(z, 2, axis=0)))
    zm1 = jnp.where(t < 1, pre1, zm1)
    zm2 = jnp.where(t < 2, pre2, zm2)
    y = cw_ref[0:1, :] * zm2 + cw_ref[1:2, :] * zm1 + cw_ref[2:3, :] * z
    b = jnp.dot(h, wb_ref[...], preferred_element_type=F32)
    a_ref[...] = (b * y).astype(BF16)

    zs_ref[...] = z[SAMPLE_ROW0:]
    tails = _prompt_tail_slices()
    zp = z[tails[0][1]:tails[0][1] + SUBLANES]
    for tile, lo in tails[1:]:
        zp = jnp.where(i == tile, z[lo:lo + SUBLANES], zp)
    zp_ref[...] = zp


def _conv_mixer(x, g, w_in, w_conv, p1, p2):
    nj = D_MODEL // CONV_TN
    col = lambda k: (lambda i, j: (0, j + k * nj))
    return pl.pallas_call(
        _conv_mixer_kernel,
        grid=(N_ROW_TILES, nj),
        in_specs=[
            pl.BlockSpec((TM, D_MODEL), lambda i, j: (i, 0)),
            _gain_spec(g),
            pl.BlockSpec((D_MODEL, CONV_TN), col(0)),
            pl.BlockSpec((D_MODEL, CONV_TN), col(1)),
            pl.BlockSpec((D_MODEL, CONV_TN), col(2)),
            pl.BlockSpec((CONV_WIDTH, CONV_TN), lambda i, j: (0, j)),
            pl.BlockSpec((M_SAMPLE, CONV_TN), lambda i, j: (0, j)),
            pl.BlockSpec((M_SAMPLE, CONV_TN), lambda i, j: (0, j)),
        ],
        out_specs=[
            pl.BlockSpec((TM, CONV_TN), lambda i, j: (i, j)),
            pl.BlockSpec((SUBLANES, CONV_TN), lambda i, j: (i, j)),
            pl.BlockSpec((M_SAMPLE, CONV_TN), lambda i, j: (0, jnp.where(i == N_ROW_TILES - 1, j, 0))),
        ],
        out_shape=[
            jax.ShapeDtypeStruct((M_ALL, D_MODEL), BF16),
            jax.ShapeDtypeStruct((N_ROW_TILES * SUBLANES, D_MODEL), F32),
            jax.ShapeDtypeStruct((M_SAMPLE, D_MODEL), F32),
        ],
        scratch_shapes=[pltpu.VMEM((TM, D_MODEL), BF16), pltpu.VMEM((nj, SUBLANES, CONV_TN), F32)],
        compiler_params=_params("arbitrary", "arbitrary"),
        name="conv_mixer",
    )(x, g[0], w_in, w_in, w_in, w_conv, p1, p2)


def _sgu_mixer_kernel(x_ref, g_ref, w_ref, lg_ref, lb_ref, ws_ref, bst_ref, wss_ref, bsts_ref,
                      a_ref, vln_ref, h_ref, mixed_ref):
    i, j = pl.program_id(0), pl.program_id(1)
    n_chunks = TM // CHUNK

    @pl.when(j == 0)
    def _():
        h = _rms(x_ref[...], g_ref[...]).astype(BF16)
        h_ref[...] = h
        v = jax.nn.gelu(jnp.dot(h, w_ref[...], preferred_element_type=F32))
        mu = jnp.mean(v, axis=-1, keepdims=True)
        xc = v - mu
        vln = xc * lax.rsqrt(jnp.mean(xc * xc, axis=-1, keepdims=True) + LN_EPS) * lg_ref[...] + lb_ref[...]
        vln_ref[...] = vln[SAMPLE_ROW0:]
        vb = vln.astype(BF16)
        r = lax.broadcasted_iota(jnp.int32, (CHUNK, CHUNK), 0)
        c = lax.broadcasted_iota(jnp.int32, (CHUNK, CHUNK), 1)
        causal = r >= c
        last_tile = i == N_ROW_TILES - 1
        for grp in range(SGU_GROUPS):
            cols = slice(grp * SGU_GROUP_DIM, (grp + 1) * SGU_GROUP_DIM)
            w_p = jnp.where(causal, ws_ref[grp], 0.0)
            w_s = jnp.where(last_tile, jnp.where(causal, wss_ref[grp], 0.0), w_p)
            b_p = bst_ref[:, grp:grp + 1]
            b_s = jnp.where(last_tile, bsts_ref[:, grp:grp + 1], b_p)
            for ch in range(n_chunks):
                rows = slice(ch * CHUNK, (ch + 1) * CHUNK)
                sample_chunk = ch == n_chunks - 1
                w = (w_s if sample_chunk else w_p).astype(BF16)
                bias = b_s if sample_chunk else b_p
                mixed_ref[rows, cols] = jnp.dot(w, vb[rows, cols], preferred_element_type=F32) + bias

    @pl.when(j == 1)
    def _():
        u = jax.nn.gelu(jnp.dot(h_ref[...], w_ref[...], preferred_element_type=F32))
        a_ref[...] = (u * mixed_ref[...]).astype(BF16)


def _sgu_mixer(x, g, w_in, ln_g, ln_b, ws, bst, ws_s, bst_s):
    fixed2 = lambda i, j: (0, 0)
    return pl.pallas_call(
        _sgu_mixer_kernel,
        grid=(N_ROW_TILES, 2),
        in_specs=[
            pl.BlockSpec((TM, D_MODEL), lambda i, j: (i, 0)),
            _gain_spec(g),
            pl.BlockSpec((D_MODEL, SGU_WIDTH), lambda i, j: (0, 1 - j)),
            pl.BlockSpec((1, SGU_WIDTH), fixed2),
            pl.BlockSpec((1, SGU_WIDTH), fixed2),
            pl.BlockSpec((SGU_GROUPS, CHUNK, CHUNK), lambda i, j: (0, 0, 0)),
            pl.BlockSpec((CHUNK, SGU_GROUPS), fixed2),
            pl.BlockSpec((SGU_GROUPS, CHUNK, CHUNK), lambda i, j: (0, 0, 0)),
            pl.BlockSpec((CHUNK, SGU_GROUPS), fixed2),
        ],
        out_specs=[
            pl.BlockSpec((TM, SGU_WIDTH), lambda i, j: (i, 0)),
            pl.BlockSpec((M_SAMPLE, SGU_WIDTH), fixed2),
        ],
        out_shape=[
            jax.ShapeDtypeStruct((M_ALL, SGU_WIDTH), BF16),
            jax.ShapeDtypeStruct((M_SAMPLE, SGU_WIDTH), F32),
        ],
        scratch_shapes=[pltpu.VMEM((TM, D_MODEL), BF16), pltpu.VMEM((TM, SGU_WIDTH), F32)],
        compiler_params=_params("arbitrary", "arbitrary"),
        name="sgu_mixer",
    )(x, g[0], w_in, ln_g, ln_b, ws, bst, ws_s, bst_s)


def _attention_layer(x, g_pre, w_qkv, sinks, cache_k, cache_v, tables):
    q, kv = _attn_proj(x, g_pre, w_qkv, tables)
    o_p, kv_tail = _attn_prompt(q, kv, sinks)

    q_s = q[M_PROMPT:].reshape(DEC_BATCH, DEC_SEQ, Q_TILES, LANES).transpose(0, 2, 1, 3)
    q_s = q_s.reshape(DEC_BATCH, SAMPLE_Q_ROWS, LANES)
    kv_new = kv[M_PROMPT:].reshape(DEC_BATCH, DEC_SEQ, 2 * KV_DIM)
    ck = cache_k.reshape(DEC_BATCH, WINDOW, KV_DIM)
    cv = cache_v.reshape(DEC_BATCH, WINDOW, KV_DIM)
    o_s, new_ks, new_vs = _attn_sample(q_s, ck, cv, kv_new, sinks)
    o_s = o_s.reshape(DEC_BATCH, Q_TILES, DEC_SEQ, LANES).transpose(0, 2, 1, 3).reshape(M_SAMPLE, D_MODEL)

    kv_shape = (BATCH, WINDOW, N_KV_HEADS, HEAD_DIM)
    new_kp = kv_tail[..., :KV_DIM].reshape(kv_shape)
    new_vp = kv_tail[..., KV_DIM:].reshape(kv_shape)
    kv_shape = (DEC_BATCH, WINDOW, N_KV_HEADS, HEAD_DIM)
    return (o_p, o_s), (new_kp, new_vp, new_ks.reshape(kv_shape), new_vs.reshape(kv_shape))


def _conv_layer(x, g_pre, w_in, w_conv, state):
    zeros = jnp.zeros((DEC_BATCH, DEC_SEQ - 1, D_MODEL), F32)
    p1 = jnp.concatenate([state[:, 1:2], zeros], axis=1).reshape(M_SAMPLE, D_MODEL)
    p2 = jnp.concatenate([state[:, 0:1], state[:, 1:2], zeros[:, 1:]], axis=1).reshape(M_SAMPLE, D_MODEL)
    a, z_p, z_s = _conv_mixer(x, g_pre, w_in, w_conv, p1, p2)
    keep = CONV_WIDTH - 1
    new_cp = jnp.stack([z_p[(tile + 1) * SUBLANES - keep:(tile + 1) * SUBLANES]
                        for tile, _ in _prompt_tail_slices()])
    new_cs = z_s.reshape(DEC_BATCH, DEC_SEQ, D_MODEL)[:, DEC_SEQ - keep:]
    return a, (new_cp, new_cs)


def _sgu_layer(x, g_pre, w_in, ln_g, ln_b, w_s, b_s):
    eye = jnp.eye(DEC_BATCH, dtype=F32)
    ws_s = jnp.einsum("ab,gts->gatbs", eye, w_s[:, :DEC_SEQ, :DEC_SEQ]).reshape(SGU_GROUPS, M_SAMPLE, M_SAMPLE)
    bst_s = jnp.tile(b_s[:, :DEC_SEQ].T, (DEC_BATCH, 1))
    a, vln_s = _sgu_mixer(x, g_pre, w_in, ln_g, ln_b, w_s, b_s.T, ws_s, bst_s)
    return a, vln_s.reshape(DEC_BATCH, DEC_SEQ, SGU_WIDTH)


def kernel(x_prompt, x_sample, cache_k, cache_v, state_conv, norm_g, ffn_w_gu, ffn_w_down,
           attn_w_qkv, attn_w_o, attn_sinks, conv_w_in, conv_w, conv_w_out,
           sgu_w_in, sgu_ln_g, sgu_ln_b, sgu_w_s, sgu_b_s, sgu_w_out):
    x = (x_prompt.reshape(M_PROMPT, D_MODEL), x_sample.reshape(M_SAMPLE, D_MODEL))
    tables = _rope_tables()
    gains = norm_g.reshape(-1, 1, D_MODEL)
    gain = lambda i, k: (gains, i * norm_g.shape[1] + k)
    mixer_weights = ((attn_w_qkv, attn_w_o), (conv_w_in, conv_w_out), (sgu_w_in, sgu_w_out))
    w_gu, w_down = ffn_w_gu[0, 0].astype(BF16), ffn_w_down[0, 0].astype(BF16)
    kp_l, vp_l, ks_l, vs_l, cp_l, cs_l, sv_l = [], [], [], [], [], [], []
    for i in range(DEPTH):
        kind, j = i % N_MIXERS, i // N_MIXERS
        w_in_f32, w_out_f32 = mixer_weights[kind]
        x, (w_in, w_out, w_gu, w_down) = _ffn(
            x, gain(i, 0), gain(i, 1), w_gu, w_down,
            casts=((w_in_f32, (j,)), (w_out_f32, (j,)), (ffn_w_gu, (i, 1)), (ffn_w_down, (i, 1))))
        if kind == 0:
            a, (kp, vp, ks, vs) = _attention_layer(x, gain(i, 2), w_in, attn_sinks[j],
                                                   cache_k[j], cache_v[j], tables)
            kp_l.append(kp); vp_l.append(vp); ks_l.append(ks); vs_l.append(vs)
        elif kind == 1:
            a, (cp, cs) = _conv_layer(x, gain(i, 2), w_in, conv_w[j], state_conv[j])
            cp_l.append(cp); cs_l.append(cs)
        else:
            a, sv = _sgu_layer(x, gain(i, 2), w_in, sgu_ln_g[j].reshape(1, SGU_WIDTH),
                               sgu_ln_b[j].reshape(1, SGU_WIDTH), sgu_w_s[j], sgu_b_s[j])
            sv_l.append(sv)
        x = _outproj(x, a, gain(i, 3), w_out)
        last = i + 1 == DEPTH
        casts = () if last else ((ffn_w_gu, (i + 1, 0)), (ffn_w_down, (i + 1, 0)))
        x, next_ffn = _ffn(x, gain(i, 4), gain(i, 5), w_gu, w_down, casts=casts, split_out=last)
        if next_ffn:
            w_gu, w_down = next_ffn
    y_prompt = x[0].reshape(BATCH, SEQ, D_MODEL)
    y_sample = x[1].reshape(DEC_BATCH, DEC_SEQ, D_MODEL)
    return (y_prompt, y_sample, jnp.stack(kp_l), jnp.stack(vp_l), jnp.stack(ks_l), jnp.stack(vs_l),
            jnp.stack(cp_l), jnp.stack(cs_l), jnp.stack(sv_l))
```

```python
import functools

import jax
import jax.numpy as jnp
from jax import lax
from jax.experimental import pallas as pl
from jax.experimental.pallas import tpu as pltpu

D_MODEL = 2048
BATCH = 2
SEQ = 4096
DEPTH = 4
DEC_BATCH = 32
DEC_SEQ = 4
PAST_LEN = 16384
HEAD_DIM = 64
N_HEADS = D_MODEL // HEAD_DIM
N_KV_HEADS = N_HEADS // 8
ROT_DIM = HEAD_DIM // 4
ROPE_THETA = 500000.0
WINDOW = 128
BLOCK = 128
CONV_WIDTH = 3
SGU_WIDTH = D_MODEL
SGU_GROUPS = 8
SGU_GROUP_DIM = SGU_WIDTH // SGU_GROUPS
CHUNK = 128
D_FF = 5632
N_MIXERS = 3
RMS_EPS = 1e-6
LN_EPS = 1e-5
NEG_INF = -1e30
LOG2E = 1.4426950408889634

M_PROMPT = BATCH * SEQ
M_SAMPLE = DEC_BATCH * DEC_SEQ
M_ALL = M_PROMPT + M_SAMPLE
KV_DIM = N_KV_HEADS * HEAD_DIM
QKV_DIM = D_MODEL + 2 * KV_DIM

LANES = 128
SUBLANES = 8
BF16_SUBLANES = 16
MXU_COLS = 256
HEADS_PER_TILE = LANES // HEAD_DIM
Q_TILES = D_MODEL // LANES
TILES_PER_KV = Q_TILES // N_KV_HEADS
KEYS = 2 * BLOCK

TM = 640
N_ROW_TILES = M_ALL // TM
SAMPLE_ROW0 = M_PROMPT - (N_ROW_TILES - 1) * TM
TF = 512
ROW_CHUNK = 16
CONV_TN = 512
VMEM_LIMIT = 58 * 1024 * 1024

assert M_ALL % TM == 0 and SAMPLE_ROW0 + M_SAMPLE == TM and M_SAMPLE == CHUNK and TM % CHUNK == 0
assert M_PROMPT % DEC_SEQ == 0 and SEQ % CHUNK == 0
assert TM % ROW_CHUNK == 0 and SAMPLE_ROW0 % ROW_CHUNK == 0 and ROW_CHUNK % BF16_SUBLANES == 0

F32 = jnp.float32
BF16 = jnp.bfloat16


def _rms(x, g):
    return x * lax.rsqrt(jnp.mean(x * x, axis=-1, keepdims=True) + RMS_EPS) * g


def _params(*sem):
    return pltpu.CompilerParams(dimension_semantics=sem, vmem_limit_bytes=VMEM_LIMIT)


def _gain_spec(gain):
    return pl.BlockSpec((None, 1, D_MODEL), lambda *_: (gain[1], 0, 0))


def _wspec(lead, block, idx):
    return pl.BlockSpec((None,) * len(lead) + block, lambda *g: lead + idx(*g))


def _row_chunks(rows):
    return [slice(c * ROW_CHUNK, (c + 1) * ROW_CHUNK) for c in range(rows // ROW_CHUNK)]


def _x_reader(x_ref, xs_ref, n_tiles, sample_row0):
    def read(rows):
        x = x_ref[rows, :]
        if xs_ref is not None and rows.start >= sample_row0:
            tail = xs_ref[rows.start - sample_row0:rows.stop - sample_row0, :]
            x = jnp.where(pl.program_id(0) == n_tiles - 1, tail, x)
        return x
    return read


def _ffn_kernel(*refs, nj, n_cast, split_in, split_out):
    refs = list(refs)
    x_ref = refs.pop(0)
    xs_ref = refs.pop(0) if split_in else None
    gpre_ref, gpost_ref, wg_ref, wu_ref, wd_ref = refs[:5]
    cast_src = refs[5:5 + n_cast]
    refs = refs[5 + n_cast:]
    o_ref = refs.pop(0)
    os_ref = refs.pop(0) if split_out else None
    cast_dst = refs[:n_cast]
    h_ref, acc_ref = refs[n_cast:]
    i, j = pl.program_id(0), pl.program_id(1)
    read_x = _x_reader(x_ref, xs_ref, N_ROW_TILES, SAMPLE_ROW0)

    @pl.when((i == 0) & (j == 0))
    def _():
        acc_ref[...] = jnp.zeros_like(acc_ref)

    @pl.when(j == 0)
    def _():
        for rows in _row_chunks(TM):
            h_ref[rows, :] = _rms(read_x(rows), gpre_ref[...]).astype(BF16)

    h = h_ref[...]
    g = jnp.dot(h, wg_ref[...], preferred_element_type=F32)
    for src, dst in zip(cast_src, cast_dst):
        dst[...] = src[...].astype(BF16)
    u = jnp.dot(h, wu_ref[...], preferred_element_type=F32)
    a = (g * jax.nn.sigmoid(g) * u).astype(BF16)
    d = jnp.dot(a, wd_ref[...], preferred_element_type=F32)
    acc_ref[...] = jnp.where(j == 0, 0.0, acc_ref[...]) + d

    @pl.when(j == nj - 1)
    def _():
        gain = 0.5 * gpost_ref[...]
        for rows in _row_chunks(TM):
            y = read_x(rows) + _rms(acc_ref[rows, :], gain)
            o_ref[rows, :] = y
            if os_ref is not None and rows.start >= SAMPLE_ROW0:
                os_ref[rows.start - SAMPLE_ROW0:rows.stop - SAMPLE_ROW0, :] = y


def _cast_rows_per_step(rows, steps):
    rps = BF16_SUBLANES
    while rows % rps or rows // rps > steps:
        rps += BF16_SUBLANES
    return rps


def _ffn(x, g_pre, g_post, w_gu, w_down, casts=(), split_out=False):
    ni, nj = N_ROW_TILES, D_FF // TF
    row_tile = pl.BlockSpec((TM, D_MODEL), lambda i, j: (i, 0))
    sample_rows = pl.BlockSpec((M_SAMPLE, D_MODEL), lambda i, j: (0, 0))
    split_in = isinstance(x, tuple)
    x_args, x_specs = (list(x), [row_tile, sample_rows]) if split_in else ([x], [row_tile])
    if split_out:
        o_specs = [row_tile, sample_rows]
        o_shapes = [jax.ShapeDtypeStruct((M_PROMPT, D_MODEL), F32), jax.ShapeDtypeStruct((M_SAMPLE, D_MODEL), F32)]
    else:
        o_specs, o_shapes = [row_tile], [jax.ShapeDtypeStruct((M_ALL, D_MODEL), F32)]
    cast_in_specs, cast_out_specs, cast_shapes, cast_args = [], [], [], []
    for w, lead in casts:
        rows, cols = w.shape[-2:]
        rps = _cast_rows_per_step(rows, ni * nj)
        slab = lambda i, j, last=rows // rps - 1: (jnp.minimum(i * nj + j, last), 0)
        cast_in_specs.append(_wspec(lead, (rps, cols), slab))
        cast_out_specs.append(pl.BlockSpec((rps, cols), slab))
        cast_shapes.append(jax.ShapeDtypeStruct((rows, cols), BF16))
        cast_args.append(w)
    out = pl.pallas_call(
        functools.partial(_ffn_kernel, nj=nj, n_cast=len(casts), split_in=split_in, split_out=split_out),
        grid=(ni, nj),
        in_specs=x_specs + [
            _gain_spec(g_pre),
            _gain_spec(g_post),
            pl.BlockSpec((D_MODEL, TF), lambda i, j: (0, j)),
            pl.BlockSpec((D_MODEL, TF), lambda i, j: (0, j + nj)),
            pl.BlockSpec((TF, D_MODEL), lambda i, j: (j, 0)),
        ] + cast_in_specs,
        out_specs=o_specs + cast_out_specs,
        out_shape=o_shapes + cast_shapes,
        scratch_shapes=[pltpu.VMEM((TM, D_MODEL), BF16), pltpu.VMEM((TM, D_MODEL), F32)],
        compiler_params=_params("arbitrary", "arbitrary"),
        name="ffn",
    )(*x_args, g_pre[0], g_post[0], w_gu, w_gu, w_down, *cast_args)
    n_x = len(o_specs)
    return (tuple(out[:n_x]) if split_out else out[0]), out[n_x:]


def _outproj_kernel(*refs, split_a):
    x_ref, a_ref = refs[:2]
    as_ref = refs[2] if split_a else None
    g_ref, w_ref, o_ref, y_ref = refs[-4:]
    a = a_ref[...]
    if as_ref is not None:
        tail = jnp.where(pl.program_id(0) == N_ROW_TILES - 1, as_ref[...], a_ref[SAMPLE_ROW0:, :])
        a = jnp.concatenate([a_ref[:SAMPLE_ROW0, :], tail], axis=0)
    y_ref[...] = jnp.dot(a, w_ref[...], preferred_element_type=F32)
    for c in range(TM // ROW_CHUNK):
        rows = slice(c * ROW_CHUNK, (c + 1) * ROW_CHUNK)
        o_ref[rows, :] = x_ref[rows, :] + _rms(y_ref[rows, :], g_ref[...])


def _outproj(x, a, g, w):
    row_tile = pl.BlockSpec((TM, D_MODEL), lambda i: (i, 0))
    split_a = isinstance(a, tuple)
    a_args = list(a) if split_a else [a]
    a_specs = [row_tile] + ([pl.BlockSpec((M_SAMPLE, D_MODEL), lambda i: (0, 0))] if split_a else [])
    return pl.pallas_call(
        functools.partial(_outproj_kernel, split_a=split_a),
        grid=(N_ROW_TILES,),
        in_specs=[row_tile] + a_specs + [
            _gain_spec(g),
            pl.BlockSpec((D_MODEL, D_MODEL), lambda i: (0, 0)),
        ],
        out_specs=row_tile,
        out_shape=jax.ShapeDtypeStruct((M_ALL, D_MODEL), F32),
        scratch_shapes=[pltpu.VMEM((TM, D_MODEL), F32)],
        compiler_params=_params("arbitrary"),
        name="outproj",
    )(x, *a_args, g[0], w)


def _rope_tables():
    inv = jnp.power(jnp.float32(ROPE_THETA), -jnp.arange(0, ROT_DIM, 2, dtype=jnp.float32) / ROT_DIM)
    pos = jnp.concatenate([
        jnp.tile(jnp.arange(SEQ), BATCH),
        jnp.tile(PAST_LEN + jnp.arange(DEC_SEQ), DEC_BATCH),
    ])
    ang = pos.astype(jnp.float32)[:, None] * inv[None, :]
    c, s = jnp.cos(ang), jnp.sin(ang)
    pad = jnp.zeros((M_ALL, HEAD_DIM - ROT_DIM), F32)
    zero = jnp.zeros_like(s)
    cos_h = jnp.concatenate([c, c, pad + 1.0], axis=1)
    sa_h = jnp.concatenate([-s, zero, pad], axis=1)
    sb_h = jnp.concatenate([zero, s, pad], axis=1)
    rep = lambda t: jnp.tile(t, (1, HEADS_PER_TILE))
    return rep(cos_h), rep(sa_h), rep(sb_h)


def _attn_proj_kernel(x_ref, g_ref, w_ref, cos_ref, sa_ref, sb_ref, q_ref, kv_ref):
    h = _rms(x_ref[...], g_ref[...]).astype(BF16)
    cos, sa, sb = cos_ref[...], sa_ref[...], sb_ref[...]
    half = ROT_DIM // 2
    for c in range(QKV_DIM // MXU_COLS):
        y2 = jnp.dot(h, w_ref[:, c * MXU_COLS:(c + 1) * MXU_COLS], preferred_element_type=F32)
        for part in range(MXU_COLS // LANES):
            t = c * (MXU_COLS // LANES) + part
            y = y2[:, part * LANES:(part + 1) * LANES]
            if t < Q_TILES + KV_DIM // LANES:
                y = y * cos + pltpu.roll(y, LANES - half, axis=1) * sa + pltpu.roll(y, half, axis=1) * sb
            if t < Q_TILES:
                q_ref[:, t * LANES:(t + 1) * LANES] = (y * (HEAD_DIM ** -0.5 * LOG2E)).astype(BF16)
            else:
                kv_ref[:, (t - Q_TILES) * LANES:(t - Q_TILES + 1) * LANES] = y


def _attn_proj(x, g, w, tables):
    row = lambda i: (i, 0)
    fixed = lambda i: (0, 0)
    return pl.pallas_call(
        _attn_proj_kernel,
        grid=(N_ROW_TILES,),
        in_specs=[
            pl.BlockSpec((TM, D_MODEL), row),
            _gain_spec(g),
            pl.BlockSpec((D_MODEL, QKV_DIM), fixed),
            pl.BlockSpec((TM, LANES), row),
            pl.BlockSpec((TM, LANES), row),
            pl.BlockSpec((TM, LANES), row),
        ],
        out_specs=[pl.BlockSpec((TM, D_MODEL), row), pl.BlockSpec((TM, 2 * KV_DIM), row)],
        out_shape=[jax.ShapeDtypeStruct((M_ALL, D_MODEL), BF16),
                   jax.ShapeDtypeStruct((M_ALL, 2 * KV_DIM), F32)],
        compiler_params=_params("arbitrary"),
        name="attn_proj",
    )(x, g[0], w, *tables)


def _both_halves(tile, first_half):
    lane = lax.broadcasted_iota(jnp.int32, tile.shape, 1)
    rolled = pltpu.roll(tile, HEAD_DIM, axis=1)
    keep = (lane < HEAD_DIM) if first_half else (lane >= HEAD_DIM)
    return jnp.where(keep, tile, rolled)


def _block_diag(rep):
    lane = lax.broadcasted_iota(jnp.int32, rep.shape, 1)
    top = jnp.where(lane < HEAD_DIM, rep, 0.0)
    bot = jnp.where(lane >= HEAD_DIM, rep, 0.0)
    return jnp.concatenate([top, bot], axis=0).astype(BF16)


def _kv_block_diag(k_all, v_all, h):
    lo = (h // HEADS_PER_TILE) * LANES
    first_half = h % HEADS_PER_TILE == 0
    return (_block_diag(_both_halves(k_all[:, lo:lo + LANES], first_half)),
            _block_diag(_both_halves(v_all[:, lo:lo + LANES], first_half)))


def _attend(q, k_bd, v_bd, mask, sinks):
    rows = mask.shape[0]
    s_all = lax.dot_general(q, k_bd, (((1,), (1,)), ((), ())), preferred_element_type=F32)
    lane = lax.broadcasted_iota(jnp.int32, (rows, LANES), 1)
    ps, sink_terms = [], []
    for t, (sink_a, sink_b) in enumerate(sinks):
        s = s_all[t * rows:(t + 1) * rows]
        sa = jnp.where(mask, s[:, :KEYS], NEG_INF)
        sb = jnp.where(mask, s[:, KEYS:], NEG_INF)
        ma = jnp.maximum(jnp.max(sa, axis=-1, keepdims=True), sink_a * LOG2E)
        mb = jnp.maximum(jnp.max(sb, axis=-1, keepdims=True), sink_b * LOG2E)
        ps.append(jnp.concatenate([jnp.exp2(sa - ma), jnp.exp2(sb - mb)], axis=1).astype(BF16))
        sink_terms.append(jnp.where(lane < HEAD_DIM, jnp.exp2(sink_a * LOG2E - ma), jnp.exp2(sink_b * LOG2E - mb)))
    key = lax.broadcasted_iota(jnp.int32, v_bd.shape, 0)
    lane_v = lax.broadcasted_iota(jnp.int32, v_bd.shape, 1)
    ones_bd = jnp.where((key < KEYS) == (lane_v < HEAD_DIM), 1.0, 0.0).astype(BF16)
    o = jnp.dot(jnp.concatenate(ps, axis=0), jnp.concatenate([v_bd, ones_bd], axis=1),
                preferred_element_type=F32)
    return o[:, :LANES] / (o[:, LANES:] + jnp.concatenate(sink_terms, axis=0))


def _window_mask(rows, q_index, min_key):
    kj = lax.broadcasted_iota(jnp.int32, (rows, KEYS), 1)
    return (kj >= q_index) & (kj <= q_index + WINDOW) & (kj >= min_key)


def _attn_prompt_kernel(sink_ref, q_ref, kvc_ref, kvp_ref, o_ref, kv_tail_ref):
    kv_tail_ref[...] = kvc_ref[...]
    min_key = jnp.where(pl.program_id(1) == 0, BLOCK, 0)
    kv = jnp.concatenate([kvp_ref[...], kvc_ref[...]], axis=0)
    qi = lax.broadcasted_iota(jnp.int32, (BLOCK, KEYS), 0)
    mask = _window_mask(BLOCK, qi, min_key)
    for h in range(N_KV_HEADS):
        k_bd, v_bd = _kv_block_diag(kv[:, :KV_DIM], kv[:, KV_DIM:], h)
        tiles = range(h * TILES_PER_KV, (h + 1) * TILES_PER_KV)
        q = jnp.concatenate([q_ref[:, t * LANES:(t + 1) * LANES] for t in tiles], axis=0)
        sinks = [(sink_ref[HEADS_PER_TILE * t], sink_ref[HEADS_PER_TILE * t + 1]) for t in tiles]
        o = _attend(q, k_bd, v_bd, mask, sinks).astype(BF16)
        for n, t in enumerate(tiles):
            o_ref[:, t * LANES:(t + 1) * LANES] = o[n * BLOCK:(n + 1) * BLOCK]


def _attn_prompt(q, kv, sinks):
    nb = SEQ // BLOCK
    return pl.pallas_call(
        _attn_prompt_kernel,
        grid=(BATCH, nb),
        in_specs=[
            pl.BlockSpec(memory_space=pltpu.SMEM),
            pl.BlockSpec((BLOCK, D_MODEL), lambda b, n: (b * nb + n, 0)),
            pl.BlockSpec((BLOCK, 2 * KV_DIM), lambda b, n: (b * nb + n, 0)),
            pl.BlockSpec((BLOCK, 2 * KV_DIM), lambda b, n: (b * nb + jnp.maximum(n - 1, 0), 0)),
        ],
        out_specs=[pl.BlockSpec((BLOCK, D_MODEL), lambda b, n: (b * nb + n, 0)),
                   pl.BlockSpec((None, BLOCK, 2 * KV_DIM), lambda b, n: (b, 0, 0))],
        out_shape=[jax.ShapeDtypeStruct((M_PROMPT, D_MODEL), BF16),
                   jax.ShapeDtypeStruct((BATCH, BLOCK, 2 * KV_DIM), F32)],
        compiler_params=_params("arbitrary", "arbitrary"),
        name="attn_prompt",
    )(sinks, q, kv, kv)


SAMPLE_Q_ROWS = Q_TILES * DEC_SEQ
ROWS_PER_KV = TILES_PER_KV * DEC_SEQ


SAMPLE_BB = 4


def _attn_sample_kernel(sink_ref, q_ref, ck_ref, cv_ref, kvn_ref, o_ref, ok_ref, ov_ref, kbuf, vbuf):
    row = lax.broadcasted_iota(jnp.int32, (ROWS_PER_KV, KEYS), 0)
    mask = _window_mask(ROWS_PER_KV, row % DEC_SEQ, 0)
    tile_of_row = lax.broadcasted_iota(jnp.int32, (ROWS_PER_KV, 1), 0) // DEC_SEQ
    sink_cols = []
    for h in range(N_KV_HEADS):
        sink_a = jnp.zeros((ROWS_PER_KV, 1), F32)
        sink_b = jnp.zeros((ROWS_PER_KV, 1), F32)
        for t in range(TILES_PER_KV):
            tile = h * TILES_PER_KV + t
            sink_a = jnp.where(tile_of_row == t, sink_ref[HEADS_PER_TILE * tile], sink_a)
            sink_b = jnp.where(tile_of_row == t, sink_ref[HEADS_PER_TILE * tile + 1], sink_b)
        sink_cols.append((sink_a, sink_b))
    pad = jnp.zeros((KEYS - WINDOW, KV_DIM), F32)
    for b in range(SAMPLE_BB):
        kbuf[b, WINDOW:, :] = pad
        vbuf[b, WINDOW:, :] = pad
        kbuf[b, :WINDOW, :] = ck_ref[b]
        vbuf[b, :WINDOW, :] = cv_ref[b]
        kbuf[b, WINDOW:WINDOW + DEC_SEQ, :] = kvn_ref[b, :, :KV_DIM]
        vbuf[b, WINDOW:WINDOW + DEC_SEQ, :] = kvn_ref[b, :, KV_DIM:]
        ok_ref[b] = kbuf[b, DEC_SEQ:DEC_SEQ + WINDOW, :]
        ov_ref[b] = vbuf[b, DEC_SEQ:DEC_SEQ + WINDOW, :]
        k_all = kbuf[b]
        v_all = vbuf[b]
        for h in range(N_KV_HEADS):
            k_bd, v_bd = _kv_block_diag(k_all, v_all, h)
            rows = slice(h * ROWS_PER_KV, (h + 1) * ROWS_PER_KV)
            o_ref[b, rows, :] = _attend(q_ref[b, rows, :], k_bd, v_bd, mask, [sink_cols[h]]).astype(BF16)


def _attn_sample(q_s, cache_k, cache_v, kv_new, sinks):
    per_b = lambda b: (b, 0, 0)
    cache_spec = pl.BlockSpec((SAMPLE_BB, WINDOW, KV_DIM), per_b)
    cache_shape = jax.ShapeDtypeStruct((DEC_BATCH, WINDOW, KV_DIM), F32)
    return pl.pallas_call(
        _attn_sample_kernel,
        grid=(DEC_BATCH // SAMPLE_BB,),
        in_specs=[
            pl.BlockSpec(memory_space=pltpu.SMEM),
            pl.BlockSpec((SAMPLE_BB, SAMPLE_Q_ROWS, LANES), per_b),
            cache_spec,
            cache_spec,
            pl.BlockSpec((SAMPLE_BB, DEC_SEQ, 2 * KV_DIM), per_b),
        ],
        out_specs=[pl.BlockSpec((SAMPLE_BB, SAMPLE_Q_ROWS, LANES), per_b), cache_spec, cache_spec],
        out_shape=[jax.ShapeDtypeStruct((DEC_BATCH, SAMPLE_Q_ROWS, LANES), BF16), cache_shape, cache_shape],
        scratch_shapes=[pltpu.VMEM((SAMPLE_BB, KEYS, KV_DIM), F32), pltpu.VMEM((SAMPLE_BB, KEYS, KV_DIM), F32)],
        compiler_params=_params("arbitrary"),
        name="attn_sample",
    )(sinks, q_s, cache_k, cache_v, kv_new)


def _prompt_tail_slices():
    out = []
    for b in range(BATCH):
        end = (b + 1) * SEQ
        tile = (end - 1) // TM
        out.append((tile, end - tile * TM - SUBLANES))
    return out


def _conv_mixer_kernel(x_ref, g_ref, wb_ref, wc_ref, wh_ref, cw_ref, p1_ref, p2_ref,
                       a_ref, zp_ref, zs_ref, h_ref, carry_ref):
    i, j = pl.program_id(0), pl.program_id(1)

    @pl.when(j == 0)
    def _():
        h_ref[...] = _rms(x_ref[...], g_ref[...]).astype(BF16)

    @pl.when(i == 0)
    def _():
        carry_ref[j] = jnp.zeros(carry_ref.shape[1:], F32)

    h = h_ref[...]
    c = jnp.dot(h, wc_ref[...], preferred_element_type=F32)
    hin = jnp.dot(h, wh_ref[...], preferred_element_type=F32)
    z = c * hin
    prev = carry_ref[j]
    carry_ref[j] = z[TM - SUBLANES:]

    row = lax.broadcasted_iota(jnp.int32, (TM, 1), 0)
    grow = row + i * TM
    is_sample = grow >= M_PROMPT
    t = jnp.where(is_sample, grow % DEC_SEQ, grow % SEQ)
    top = jnp.zeros((SAMPLE_ROW0, z.shape[1]), F32)
    pre1 = jnp.where(is_sample, jnp.concatenate([top, p1_ref[...]], axis=0), 0.0)
    pre2 = jnp.where(is_sample, jnp.concatenate([top, p2_ref[...]], axis=0), 0.0)
    last = prev[SUBLANES - 1:SUBLANES]
    zm1 = jnp.where(row == 0, last, pltpu.roll(z, 1, axis=0))
    zm2 = jnp.where(row == 0, prev[SUBLANES - 2:SUBLANES - 1],
                    jnp.where(row == 1, last, pltpu.roll(z, 2, axis=0)))
    zm1 = jnp.where(t < 1, pre1, zm1)
    zm2 = jnp.where(t < 2, pre2, zm2)
    y = cw_ref[0:1, :] * zm2 + cw_ref[1:2, :] * zm1 + cw_ref[2:3, :] * z
    b = jnp.dot(h, wb_ref[...], preferred_element_type=F32)
    a_ref[...] = (b * y).astype(BF16)

    zs_ref[...] = z[SAMPLE_ROW0:]
    tails = _prompt_tail_slices()
    zp = z[tails[0][1]:tails[0][1] + SUBLANES]
    for tile, lo in tails[1:]:
        zp = jnp.where(i == tile, z[lo:lo + SUBLANES], zp)
    zp_ref[...] = zp


def _conv_mixer(x, g, w_in, w_conv, p1, p2):
    nj = D_MODEL // CONV_TN
    col = lambda k: (lambda i, j: (0, j + k * nj))
    return pl.pallas_call(
        _conv_mixer_kernel,
        grid=(N_ROW_TILES, nj),
        in_specs=[
            pl.BlockSpec((TM, D_MODEL), lambda i, j: (i, 0)),
            _gain_spec(g),
            pl.BlockSpec((D_MODEL, CONV_TN), col(0)),
            pl.BlockSpec((D_MODEL, CONV_TN), col(1)),
            pl.BlockSpec((D_MODEL, CONV_TN), col(2)),
            pl.BlockSpec((CONV_WIDTH, CONV_TN), lambda i, j: (0, j)),
            pl.BlockSpec((M_SAMPLE, CONV_TN), lambda i, j: (0, j)),
            pl.BlockSpec((M_SAMPLE, CONV_TN), lambda i, j: (0, j)),
        ],
        out_specs=[
            pl.BlockSpec((TM, CONV_TN), lambda i, j: (i, j)),
            pl.BlockSpec((SUBLANES, CONV_TN), lambda i, j: (i, j)),
            pl.BlockSpec((M_SAMPLE, CONV_TN), lambda i, j: (0, jnp.where(i == N_ROW_TILES - 1, j, 0))),
        ],
        out_shape=[
            jax.ShapeDtypeStruct((M_ALL, D_MODEL), BF16),
            jax.ShapeDtypeStruct((N_ROW_TILES * SUBLANES, D_MODEL), F32),
            jax.ShapeDtypeStruct((M_SAMPLE, D_MODEL), F32),
        ],
        scratch_shapes=[pltpu.VMEM((TM, D_MODEL), BF16), pltpu.VMEM((nj, SUBLANES, CONV_TN), F32)],
        compiler_params=_params("arbitrary", "arbitrary"),
        name="conv_mixer",
    )(x, g[0], w_in, w_in, w_in, w_conv, p1, p2)


def _sgu_mixer_kernel(x_ref, g_ref, w_ref, lg_ref, lb_ref, ws_ref, bst_ref, wss_ref, bsts_ref,
                      a_ref, vln_ref, h_ref, mixed_ref):
    i = pl.program_id(0)
    n_chunks = TM // CHUNK

    def mix():
        h = _rms(x_ref[...], g_ref[...]).astype(BF16)
        h_ref[...] = h
        v = jax.nn.gelu(jnp.dot(h, w_ref[:, SGU_WIDTH:], preferred_element_type=F32))
        mu = jnp.mean(v, axis=-1, keepdims=True)
        xc = v - mu
        vln = xc * lax.rsqrt(jnp.mean(xc * xc, axis=-1, keepdims=True) + LN_EPS) * lg_ref[...] + lb_ref[...]
        vln_ref[...] = vln[SAMPLE_ROW0:]
        vb = vln.astype(BF16)
        r = lax.broadcasted_iota(jnp.int32, (CHUNK, CHUNK), 0)
        c = lax.broadcasted_iota(jnp.int32, (CHUNK, CHUNK), 1)
        causal = r >= c
        last_tile = i == N_ROW_TILES - 1
        for grp in range(SGU_GROUPS):
            cols = slice(grp * SGU_GROUP_DIM, (grp + 1) * SGU_GROUP_DIM)
            w_p = jnp.where(causal, ws_ref[grp], 0.0)
            w_s = jnp.where(last_tile, jnp.where(causal, wss_ref[grp], 0.0), w_p)
            b_p = bst_ref[:, grp:grp + 1]
            b_s = jnp.where(last_tile, bsts_ref[:, grp:grp + 1], b_p)
            for ch in range(n_chunks):
                rows = slice(ch * CHUNK, (ch + 1) * CHUNK)
                sample_chunk = ch == n_chunks - 1
                w = (w_s if sample_chunk else w_p).astype(BF16)
                bias = b_s if sample_chunk else b_p
                mixed_ref[rows, cols] = jnp.dot(w, vb[rows, cols], preferred_element_type=F32) + bias

    def gate():
        u = jax.nn.gelu(jnp.dot(h_ref[...], w_ref[:, :SGU_WIDTH], preferred_element_type=F32))
        a_ref[...] = (u * mixed_ref[...]).astype(BF16)

    mix()
    gate()


def _sgu_mixer(x, g, w_in, ln_g, ln_b, ws, bst, ws_s, bst_s):
    fixed2 = lambda i: (0, 0)
    return pl.pallas_call(
        _sgu_mixer_kernel,
        grid=(N_ROW_TILES,),
        in_specs=[
            pl.BlockSpec((TM, D_MODEL), lambda i: (i, 0)),
            _gain_spec(g),
            pl.BlockSpec((D_MODEL, 2 * SGU_WIDTH), fixed2, pipeline_mode=pl.Buffered(1)),
            pl.BlockSpec((1, SGU_WIDTH), fixed2),
            pl.BlockSpec((1, SGU_WIDTH), fixed2),
            pl.BlockSpec((SGU_GROUPS, CHUNK, CHUNK), lambda i: (0, 0, 0)),
            pl.BlockSpec((CHUNK, SGU_GROUPS), fixed2),
            pl.BlockSpec((SGU_GROUPS, CHUNK, CHUNK), lambda i: (0, 0, 0)),
            pl.BlockSpec((CHUNK, SGU_GROUPS), fixed2),
        ],
        out_specs=[
            pl.BlockSpec((TM, SGU_WIDTH), lambda i: (i, 0)),
            pl.BlockSpec((M_SAMPLE, SGU_WIDTH), fixed2),
        ],
        out_shape=[
            jax.ShapeDtypeStruct((M_ALL, SGU_WIDTH), BF16),
            jax.ShapeDtypeStruct((M_SAMPLE, SGU_WIDTH), F32),
        ],
        scratch_shapes=[pltpu.VMEM((TM, D_MODEL), BF16), pltpu.VMEM((TM, SGU_WIDTH), F32)],
        compiler_params=_params("arbitrary"),
        name="sgu_mixer",
    )(x, g[0], w_in, ln_g, ln_b, ws, bst, ws_s, bst_s)


def _attention_layer(x, g_pre, w_qkv, sinks, cache_k, cache_v, tables):
    q, kv = _attn_proj(x, g_pre, w_qkv, tables)
    o_p, kv_tail = _attn_prompt(q, kv, sinks)

    q_s = q[M_PROMPT:].reshape(DEC_BATCH, DEC_SEQ, Q_TILES, LANES).transpose(0, 2, 1, 3)
    q_s = q_s.reshape(DEC_BATCH, SAMPLE_Q_ROWS, LANES)
    kv_new = kv[M_PROMPT:].reshape(DEC_BATCH, DEC_SEQ, 2 * KV_DIM)
    ck = cache_k.reshape(DEC_BATCH, WINDOW, KV_DIM)
    cv = cache_v.reshape(DEC_BATCH, WINDOW, KV_DIM)
    o_s, new_ks, new_vs = _attn_sample(q_s, ck, cv, kv_new, sinks)
    o_s = o_s.reshape(DEC_BATCH, Q_TILES, DEC_SEQ, LANES).transpose(0, 2, 1, 3).reshape(M_SAMPLE, D_MODEL)

    kv_shape = (BATCH, WINDOW, N_KV_HEADS, HEAD_DIM)
    new_kp = kv_tail[..., :KV_DIM].reshape(kv_shape)
    new_vp = kv_tail[..., KV_DIM:].reshape(kv_shape)
    kv_shape = (DEC_BATCH, WINDOW, N_KV_HEADS, HEAD_DIM)
    return (o_p, o_s), (new_kp, new_vp, new_ks.reshape(kv_shape), new_vs.reshape(kv_shape))


def _conv_layer(x, g_pre, w_in, w_conv, state):
    zeros = jnp.zeros((DEC_BATCH, DEC_SEQ - 1, D_MODEL), F32)
    p1 = jnp.concatenate([state[:, 1:2], zeros], axis=1).reshape(M_SAMPLE, D_MODEL)
    p2 = jnp.concatenate([state[:, 0:1], state[:, 1:2], zeros[:, 1:]], axis=1).reshape(M_SAMPLE, D_MODEL)
    a, z_p, z_s = _conv_mixer(x, g_pre, w_in, w_conv, p1, p2)
    keep = CONV_WIDTH - 1
    new_cp = jnp.stack([z_p[(tile + 1) * SUBLANES - keep:(tile + 1) * SUBLANES]
                        for tile, _ in _prompt_tail_slices()])
    new_cs = z_s.reshape(DEC_BATCH, DEC_SEQ, D_MODEL)[:, DEC_SEQ - keep:]
    return a, (new_cp, new_cs)


def _sgu_layer(x, g_pre, w_in, ln_g, ln_b, w_s, b_s):
    eye = jnp.eye(DEC_BATCH, dtype=F32)
    ws_s = jnp.einsum("ab,gts->gatbs", eye, w_s[:, :DEC_SEQ, :DEC_SEQ]).reshape(SGU_GROUPS, M_SAMPLE, M_SAMPLE)
    bst_s = jnp.tile(b_s[:, :DEC_SEQ].T, (DEC_BATCH, 1))
    a, vln_s = _sgu_mixer(x, g_pre, w_in, ln_g, ln_b, w_s, b_s.T, ws_s, bst_s)
    return a, vln_s.reshape(DEC_BATCH, DEC_SEQ, SGU_WIDTH)


def kernel(x_prompt, x_sample, cache_k, cache_v, state_conv, norm_g, ffn_w_gu, ffn_w_down,
           attn_w_qkv, attn_w_o, attn_sinks, conv_w_in, conv_w, conv_w_out,
           sgu_w_in, sgu_ln_g, sgu_ln_b, sgu_w_s, sgu_b_s, sgu_w_out):
    x = (x_prompt.reshape(M_PROMPT, D_MODEL), x_sample.reshape(M_SAMPLE, D_MODEL))
    tables = _rope_tables()
    gains = norm_g.reshape(-1, 1, D_MODEL)
    gain = lambda i, k: (gains, i * norm_g.shape[1] + k)
    mixer_weights = ((attn_w_qkv, attn_w_o), (conv_w_in, conv_w_out), (sgu_w_in, sgu_w_out))
    w_gu, w_down = ffn_w_gu[0, 0].astype(BF16), ffn_w_down[0, 0].astype(BF16)
    kp_l, vp_l, ks_l, vs_l, cp_l, cs_l, sv_l = [], [], [], [], [], [], []
    for i in range(DEPTH):
        kind, j = i % N_MIXERS, i // N_MIXERS
        w_in_f32, w_out_f32 = mixer_weights[kind]
        x, (w_in, w_out, w_gu, w_down) = _ffn(
            x, gain(i, 0), gain(i, 1), w_gu, w_down,
            casts=((w_in_f32, (j,)), (w_out_f32, (j,)), (ffn_w_gu, (i, 1)), (ffn_w_down, (i, 1))))
        if kind == 0:
            a, (kp, vp, ks, vs) = _attention_layer(x, gain(i, 2), w_in, attn_sinks[j],
                                                   cache_k[j], cache_v[j], tables)
            kp_l.append(kp); vp_l.append(vp); ks_l.append(ks); vs_l.append(vs)
        elif kind == 1:
            a, (cp, cs) = _conv_layer(x, gain(i, 2), w_in, conv_w[j], state_conv[j])
            cp_l.append(cp); cs_l.append(cs)
        else:
            a, sv = _sgu_layer(x, gain(i, 2), w_in, sgu_ln_g[j].reshape(1, SGU_WIDTH),
                               sgu_ln_b[j].reshape(1, SGU_WIDTH), sgu_w_s[j], sgu_b_s[j])
            sv_l.append(sv)
        x = _outproj(x, a, gain(i, 3), w_out)
        last = i + 1 == DEPTH
        casts = () if last else ((ffn_w_gu, (i + 1, 0)), (ffn_w_down, (i + 1, 0)))
        x, next_ffn = _ffn(x, gain(i, 4), gain(i, 5), w_gu, w_down, casts=casts, split_out=last)
        if next_ffn:
            w_gu, w_down = next_ffn
    y_prompt = x[0].reshape(BATCH, SEQ, D_MODEL)
    y_sample = x[1].reshape(DEC_BATCH, DEC_SEQ, D_MODEL)
    return (y_prompt, y_sample, jnp.stack(kp_l), jnp.stack(vp_l), jnp.stack(ks_l), jnp.stack(vs_l),
            jnp.stack(cp_l), jnp.stack(cs_l), jnp.stack(sv_l))
```
